```python
import math
import jax, jax.numpy as jnp
from jax import lax
import numpy as np

D_MODEL = 4096
BATCH = 1
SEQ = 16384
DEPTH = 4

CHUNK = 64
QBLOCK = 128
N_A_LAYERS = DEPTH // 2
N_B_LAYERS = DEPTH - N_A_LAYERS
DH_A = 128
H_A = D_MODEL // DH_A
DH_B = 128
H_B = D_MODEL // (2 * DH_B)
DV_B = 2 * DH_B
D_FF = 3 * D_MODEL // 2
EPS = 1e-6
FGATE_BIAS = 3.0

kernel_name = "fox_diffattn_yoco_macaron"


def rms_norm(x, g):
    xf = x.astype(jnp.float32)
    y = xf * lax.rsqrt(jnp.mean(xf * xf, axis=-1, keepdims=True) + EPS)
    return (y * g.astype(jnp.float32)).astype(x.dtype)


def swiglu(x, w_in, w_out):
    g, u = jnp.split(x @ w_in, 2, axis=-1)
    return (jax.nn.silu(g) * u) @ w_out


def _to_blocks(a):
    b, s = a.shape[:2]
    a = a.reshape((b, s // QBLOCK, QBLOCK) + a.shape[2:])
    return jnp.moveaxis(a, 1, 0)


def _from_blocks(a):
    a = jnp.moveaxis(a, 0, 1)
    return a.reshape((a.shape[0], a.shape[1] * a.shape[2]) + a.shape[3:])


def forgetting_attention(q, k, v, log_f):
    s_len, dh = q.shape[1], q.shape[-1]
    scale = dh ** -0.5
    dcum = jnp.cumsum(log_f, axis=1)
    dcum_k = jnp.swapaxes(dcum, 1, 2)
    pos = jnp.arange(s_len)
    qpos = pos.reshape(-1, QBLOCK)

    def block(args):
        q_blk, d_blk, p_blk = args
        logits = jnp.einsum('bqhd,bkhd->bhqk', q_blk, k,
                            preferred_element_type=jnp.float32) * scale
        logits = logits + jnp.swapaxes(d_blk, 1, 2)[..., None] - dcum_k[:, :, None, :]
        causal = pos[None, :] <= p_blk[:, None]
        logits = jnp.where(causal, logits, -jnp.inf)
        p = jax.nn.softmax(logits, axis=-1)
        return jnp.einsum('bhqk,bkhd->bqhd', p.astype(v.dtype), v)

    out = lax.map(block, (_to_blocks(q), _to_blocks(dcum), qpos))
    return _from_blocks(out)


def differential_attention(q, k, v, lam):
    s_len, n_heads, dh = q.shape[1], q.shape[2], q.shape[-1]
    scale = dh ** -0.5
    slopes = jnp.exp2(-8.0 * jnp.arange(1, n_heads + 1, dtype=jnp.float32) / n_heads)
    pos = jnp.arange(s_len)
    qpos = pos.reshape(-1, QBLOCK)

    def block(args):
        q_blk, p_blk = args
        dist = jnp.abs(p_blk[:, None] - pos[None, :]).astype(jnp.float32)
        visible = (pos[None, :] // CHUNK) <= (p_blk[:, None] // CHUNK)
        bias = jnp.where(visible, -slopes[:, None, None] * dist, -jnp.inf)
        logits = jnp.einsum('bqhmd,bkhmd->bmhqk', q_blk, k,
                            preferred_element_type=jnp.float32) * scale + bias
        attn = jax.nn.softmax(logits, axis=-1)
        w = attn[:, 0] - lam * attn[:, 1]
        return jnp.einsum('bhqk,bkhe->bqhe', w.astype(v.dtype), v)

    return _from_blocks(lax.map(block, (_to_blocks(q), qpos)))


def fox_mixer(hn, w_in, b_f, w_out):
    b, s, _ = hn.shape
    width = H_A * DH_A
    q, k, v, f = jnp.split(hn @ w_in, [width, 2 * width, 3 * width], axis=-1)
    shp = (b, s, H_A, DH_A)
    log_f = jax.nn.log_sigmoid((f + b_f).astype(jnp.float32))
    o = forgetting_attention(q.reshape(shp), k.reshape(shp), v.reshape(shp), log_f)
    return o.reshape(b, s, width) @ w_out


def diff_mixer(hn, k_sh, v_sh, w_q, lam_vecs, g_subln, w_out, layer_idx):
    b, s, _ = hn.shape
    lam_init = 0.8 - 0.6 * math.exp(-0.3 * (layer_idx - 1))
    lv = lam_vecs.astype(jnp.float32)
    lam = jnp.exp(jnp.sum(lv[0] * lv[1])) - jnp.exp(jnp.sum(lv[2] * lv[3])) + lam_init
    q = (hn @ w_q).reshape(b, s, H_B, 2, DH_B)
    o = differential_attention(q, k_sh, v_sh, lam)
    o = rms_norm(o, g_subln) * (1.0 - lam_init)
    return o.reshape(b, s, H_B * DV_B) @ w_out


def setup_inputs(seed: int = 0) -> dict:
    key = jax.random.key(seed)
    ks = jax.random.split(key, 16)
    f32 = jnp.float32

    def nrm(k, shape, fan_in):
        return jax.random.normal(k, shape, f32) * (fan_in ** -0.5)

    def gain(k, shape):
        return 1.0 + 0.02 * jax.random.normal(k, shape, f32)

    qkv_a = 3 * H_A * DH_A + H_A
    return {
        "x": jax.random.normal(ks[0], (BATCH, SEQ, D_MODEL), f32),
        "g_ffn": gain(ks[1], (DEPTH, 2, D_MODEL)),
        "w_ffn_in": nrm(ks[2], (DEPTH, 2, D_MODEL, 2 * D_FF), D_MODEL),
        "w_ffn_out": nrm(ks[3], (DEPTH, 2, D_FF, D_MODEL), D_FF),
        "g_mix": gain(ks[4], (DEPTH, D_MODEL)),
        "w_in_a": nrm(ks[5], (N_A_LAYERS, D_MODEL, qkv_a), D_MODEL),
        "b_f": FGATE_BIAS + 0.1 * jax.random.normal(ks[6], (N_A_LAYERS, H_A), f32),
        "w_out_a": nrm(ks[7], (N_A_LAYERS, H_A * DH_A, D_MODEL), H_A * DH_A),
        "g_kv": gain(ks[8], (D_MODEL,)),
        "w_kv": nrm(ks[9], (D_MODEL, H_B * 2 * DH_B + H_B * DV_B), D_MODEL),
        "w_q_b": nrm(ks[10], (N_B_LAYERS, D_MODEL, H_B * 2 * DH_B), D_MODEL),
        "lam_b": 0.1 * jax.random.normal(ks[11], (N_B_LAYERS, 4, DH_B), f32),
        "g_subln": gain(ks[12], (N_B_LAYERS, DV_B)),
        "w_out_b": nrm(ks[13], (N_B_LAYERS, H_B * DV_B, D_MODEL), H_B * DV_B),
        "g_final": gain(ks[14], (D_MODEL,)),
    }


def reference(x, g_ffn, w_ffn_in, w_ffn_out, g_mix, w_in_a, b_f, w_out_a,
              g_kv, w_kv, w_q_b, lam_b, g_subln, w_out_b, g_final):
    b, s, _ = x.shape
    h = x
    k_sh = None
    v_sh = None
    for l in range(DEPTH):
        h = h + 0.5 * swiglu(rms_norm(h, g_ffn[l, 0]), w_ffn_in[l, 0], w_ffn_out[l, 0])
        hn = rms_norm(h, g_mix[l])
        if l < N_A_LAYERS:
            h = h + fox_mixer(hn, w_in_a[l], b_f[l], w_out_a[l])
        else:
            j = l - N_A_LAYERS
            h = h + diff_mixer(hn, k_sh, v_sh, w_q_b[j], lam_b[j], g_subln[j],
                               w_out_b[j], l + 1)
        h = h + 0.5 * swiglu(rms_norm(h, g_ffn[l, 1]), w_ffn_in[l, 1], w_ffn_out[l, 1])
        if l == N_A_LAYERS - 1:
            kv = rms_norm(h, g_kv) @ w_kv
            k_flat, v_flat = jnp.split(kv, [H_B * 2 * DH_B], axis=-1)
            k_sh = k_flat.reshape(b, s, H_B, 2, DH_B)
            v_sh = v_flat.reshape(b, s, H_B, DV_B)
    return rms_norm(h, g_final)
```

```python
import functools
import math

import jax
import jax.numpy as jnp
from jax import lax
from jax.experimental import pallas as pl
from jax.experimental.pallas import tpu as pltpu

F32 = jnp.float32
BF16 = jnp.bfloat16

EPS = 1e-6
HEAD_DIM = 128
CHUNK = 64
LOG2E = 1.4426950408889634
NEG_INF = float("-inf")

V7X_VMEM_LIMIT_BYTES = 56 * 1024 * 1024
V7X_LANES = 128


def _params(*sem):
    return pltpu.CompilerParams(dimension_semantics=sem, vmem_limit_bytes=V7X_VMEM_LIMIT_BYTES)


def _tile(dim, want):
    t = min(dim, want)
    while dim % t:
        t //= 2
    return t


def _rmsnorm_kernel(x_ref, g_ref, o_ref):
    x = x_ref[...]
    ms = jnp.mean(x * x, axis=-1, keepdims=True)
    o_ref[...] = (x * lax.rsqrt(ms + EPS) * g_ref[...]).astype(o_ref.dtype)


def rmsnorm(x, g, out_dtype, rows=512):
    s, d = x.shape
    tr = _tile(s, rows)
    return pl.pallas_call(
        _rmsnorm_kernel,
        out_shape=jax.ShapeDtypeStruct((s, d), out_dtype),
        grid=(s // tr,),
        in_specs=[pl.BlockSpec((tr, d), lambda i: (i, 0)),
                  pl.BlockSpec((1, d), lambda i: (0, 0))],
        out_specs=pl.BlockSpec((tr, d), lambda i: (i, 0)),
        compiler_params=_params("parallel"),
        name="rmsnorm",
    )(x, g.reshape(1, d).astype(F32))


def _mm_kernel(a_ref, w_ref, o_ref):
    o_ref[...] = jnp.dot(a_ref[...], w_ref[...], preferred_element_type=F32).astype(o_ref.dtype)


def matmul(a, w, out_dtype, tm=1024, tn=512):
    m, k = a.shape
    n = w.shape[1]
    tm, tn = _tile(m, tm), _tile(n, tn)
    return pl.pallas_call(
        _mm_kernel,
        out_shape=jax.ShapeDtypeStruct((m, n), out_dtype),
        grid=(m // tm, n // tn),
        in_specs=[pl.BlockSpec((tm, k), lambda i, j: (i, 0)),
                  pl.BlockSpec((k, tn), lambda i, j: (0, j))],
        out_specs=pl.BlockSpec((tm, tn), lambda i, j: (i, j)),
        compiler_params=_params("parallel", "arbitrary"),
        name="matmul",
    )(a, w)


def _mm_resid_kernel(a_ref, w_ref, r_ref, o_ref, *, alpha):
    acc = jnp.dot(a_ref[...], w_ref[...], preferred_element_type=F32)
    o_ref[...] = r_ref[...] + alpha * acc


def matmul_residual(a, w, resid, alpha, tm=1024, tn=512):
    m, k = a.shape
    n = w.shape[1]
    tm, tn = _tile(m, tm), _tile(n, tn)
    return pl.pallas_call(
        functools.partial(_mm_resid_kernel, alpha=alpha),
        out_shape=jax.ShapeDtypeStruct((m, n), F32),
        grid=(m // tm, n // tn),
        in_specs=[pl.BlockSpec((tm, k), lambda i, j: (i, 0)),
                  pl.BlockSpec((k, tn), lambda i, j: (0, j)),
                  pl.BlockSpec((tm, tn), lambda i, j: (i, j))],
        out_specs=pl.BlockSpec((tm, tn), lambda i, j: (i, j)),
        compiler_params=_params("parallel", "arbitrary"),
        name="matmul_residual",
    )(a, w, resid)


def _swiglu_kernel(a_ref, wg_ref, wu_ref, o_ref):
    a = a_ref[...]
    g = jnp.dot(a, wg_ref[...], preferred_element_type=F32)
    u = jnp.dot(a, wu_ref[...], preferred_element_type=F32)
    o_ref[...] = (g / (1.0 + jnp.exp(-g)) * u).astype(o_ref.dtype)


def swiglu_in(a, w_in, tm=1024, tn=512):
    m, k = a.shape
    f = w_in.shape[1] // 2
    tm, tn = _tile(m, tm), _tile(f, tn)
    nj = f // tn
    return pl.pallas_call(
        _swiglu_kernel,
        out_shape=jax.ShapeDtypeStruct((m, f), BF16),
        grid=(m // tm, nj),
        in_specs=[pl.BlockSpec((tm, k), lambda i, j: (i, 0)),
                  pl.BlockSpec((k, tn), lambda i, j: (0, j)),
                  pl.BlockSpec((k, tn), lambda i, j: (0, j + nj))],
        out_specs=pl.BlockSpec((tm, tn), lambda i, j: (i, j)),
        compiler_params=_params("parallel", "arbitrary"),
        name="swiglu_in",
    )(a, w_in, w_in)


def _split3(x):
    hi = x.astype(BF16)
    r1 = x - hi.astype(F32)
    mid = r1.astype(BF16)
    lo = (r1 - mid.astype(F32)).astype(BF16)
    return hi, mid, lo


def _gate_cumsum_kernel(f_ref, b_ref, o_ref, carry_ref):
    @pl.when(pl.program_id(0) == 0)
    def _():
        carry_ref[...] = jnp.zeros_like(carry_ref)

    x = f_ref[...] + b_ref[...]
    logf = jnp.minimum(x, 0.0) - jnp.log(1.0 + jnp.exp(-jnp.abs(x)))
    r = x.shape[0]
    row = lax.broadcasted_iota(jnp.int32, (r, r), 0)
    col = lax.broadcasted_iota(jnp.int32, (r, r), 1)
    tri = (col <= row).astype(BF16)
    hi, mid, lo = _split3(logf)
    csum = (jnp.dot(tri, lo, preferred_element_type=F32)
            + jnp.dot(tri, mid, preferred_element_type=F32)
            + jnp.dot(tri, hi, preferred_element_type=F32))
    tot = carry_ref[...] + csum
    o_ref[...] = tot * (-LOG2E)
    carry_ref[...] = tot[r - 1:r, :]


def gate_cumsum(f, b, rows=256):
    s, w = f.shape
    tr = _tile(s, rows)
    return pl.pallas_call(
        _gate_cumsum_kernel,
        out_shape=jax.ShapeDtypeStruct((s, w), F32),
        grid=(s // tr,),
        in_specs=[pl.BlockSpec((tr, w), lambda i: (i, 0)),
                  pl.BlockSpec((1, w), lambda i: (0, 0))],
        out_specs=pl.BlockSpec((tr, w), lambda i: (i, 0)),
        scratch_shapes=[pltpu.VMEM((1, w), F32)],
        compiler_params=_params("arbitrary"),
        name="gate_cumsum",
    )(f, b)


def _fox_kernel(q_ref, kt_ref, v_ref, nd_ref, o_ref, m_ref, l_ref, acc_ref, *, t):
    i = pl.program_id(1)
    c = (HEAD_DIM ** -0.5) * LOG2E
    q = q_ref[...]
    m_ref[...] = jnp.full_like(m_ref, NEG_INF)
    l_ref[...] = jnp.zeros_like(l_ref)
    acc_ref[...] = jnp.zeros_like(acc_ref)

    def step(j, masked):
        s = jnp.dot(q, kt_ref[j], preferred_element_type=F32) * c + nd_ref[j]
        if masked:
            row = lax.broadcasted_iota(jnp.int32, (t, t), 0)
            col = lax.broadcasted_iota(jnp.int32, (t, t), 1)
            s = jnp.where(col <= row, s, NEG_INF)
        m_prev = m_ref[...]
        m_new = jnp.maximum(m_prev, jnp.max(s, axis=-1, keepdims=True))
        alpha = jnp.exp2(m_prev - m_new)
        p = jnp.exp2(s - m_new)
        l_ref[...] = alpha * l_ref[...] + jnp.sum(p, axis=-1, keepdims=True)
        acc_ref[...] = alpha * acc_ref[...] + jnp.dot(p.astype(BF16), v_ref[j],
                                                      preferred_element_type=F32)
        m_ref[...] = m_new

    def body(j, carry):
        step(j, False)
        return carry

    lax.fori_loop(0, i, body, 0)
    step(i, True)
    o_ref[...] = (acc_ref[...] / l_ref[...]).astype(o_ref.dtype)


def fox_attention(qh, kt, vh, nd, t):
    h, s, dh = qh.shape
    nb = s // t
    return pl.pallas_call(
        functools.partial(_fox_kernel, t=t),
        out_shape=jax.ShapeDtypeStruct((s, h * dh), BF16),
        grid=(h, nb),
        in_specs=[pl.BlockSpec((None, t, dh), lambda hh, i: (hh, i, 0)),
                  pl.BlockSpec((None, nb, dh, t), lambda hh, i: (hh, 0, 0, 0)),
                  pl.BlockSpec((None, nb, t, dh), lambda hh, i: (hh, 0, 0, 0)),
                  pl.BlockSpec((None, nb, 1, t), lambda hh, i: (hh, 0, 0, 0))],
        out_specs=pl.BlockSpec((t, dh), lambda hh, i: (i, hh)),
        scratch_shapes=[pltpu.VMEM((t, 1), F32), pltpu.VMEM((t, 1), F32),
                        pltpu.VMEM((t, dh), F32)],
        compiler_params=_params("parallel", "arbitrary"),
        name="fox_attention",
    )(qh, kt, vh, nd)


def _diff_kernel(slope_ref, q_ref, kt_ref, v_ref, lam_ref, g_ref, o_ref,
                 m_ref, l_ref, acc_ref, *, t, lam_init):
    hh = pl.program_id(0)
    i = pl.program_id(1)
    c = (HEAD_DIM ** -0.5) * LOG2E
    slope2 = slope_ref[hh] * LOG2E
    m_ref[...] = jnp.full_like(m_ref, NEG_INF)
    l_ref[...] = jnp.zeros_like(l_ref)
    acc_ref[...] = jnp.zeros_like(acc_ref)
    row = lax.broadcasted_iota(jnp.int32, (t, t), 0)
    col = lax.broadcasted_iota(jnp.int32, (t, t), 1)

    def step(j, masked):
        dist = jnp.abs((i - j) * t + row - col).astype(F32)
        bias = dist * (-slope2)
        if masked:
            bias = jnp.where(col // CHUNK <= row // CHUNK, bias, NEG_INF)
        v = v_ref[j]
        for mp in range(2):
            s = jnp.dot(q_ref[mp], kt_ref[mp, j], preferred_element_type=F32) * c + bias
            m_prev = m_ref[mp]
            m_new = jnp.maximum(m_prev, jnp.max(s, axis=-1, keepdims=True))
            alpha = jnp.exp2(m_prev - m_new)
            p = jnp.exp2(s - m_new)
            l_ref[mp] = alpha * l_ref[mp] + jnp.sum(p, axis=-1, keepdims=True)
            acc_ref[mp] = alpha * acc_ref[mp] + jnp.dot(p.astype(BF16), v,
                                                        preferred_element_type=F32)
            m_ref[mp] = m_new

    def body(j, carry):
        step(j, False)
        return carry

    lax.fori_loop(0, i, body, 0)
    step(i, True)

    lv = lam_ref[...]
    lam = (jnp.exp(jnp.sum(lv[0:1] * lv[1:2], axis=-1, keepdims=True))
           - jnp.exp(jnp.sum(lv[2:3] * lv[3:4], axis=-1, keepdims=True)) + lam_init)
    o = acc_ref[0] / l_ref[0] - lam * (acc_ref[1] / l_ref[1])
    ms = jnp.mean(o * o, axis=-1, keepdims=True)
    o_ref[...] = (o * lax.rsqrt(ms + EPS) * g_ref[...] * (1.0 - lam_init)).astype(o_ref.dtype)


def diff_attention(qh, kt, vh, lam_vecs, g_subln, t, lam_init):
    h, _, s, dh = qh.shape
    dv = vh.shape[-1]
    nb = s // t
    slopes = jnp.exp2(-8.0 * jnp.arange(1, h + 1, dtype=F32) / h)
    return pl.pallas_call(
        functools.partial(_diff_kernel, t=t, lam_init=lam_init),
        out_shape=jax.ShapeDtypeStruct((s, h * dv), BF16),
        grid=(h, nb),
        in_specs=[pl.BlockSpec(memory_space=pltpu.SMEM),
                  pl.BlockSpec((None, 2, t, dh), lambda hh, i: (hh, 0, i, 0)),
                  pl.BlockSpec((None, 2, nb, dh, t), lambda hh, i: (hh, 0, 0, 0, 0)),
                  pl.BlockSpec((None, nb, t, dv), lambda hh, i: (hh, 0, 0, 0)),
                  pl.BlockSpec((4, dh), lambda hh, i: (0, 0)),
                  pl.BlockSpec((1, dv), lambda hh, i: (0, 0))],
        out_specs=pl.BlockSpec((t, dv), lambda hh, i: (i, hh)),
        scratch_shapes=[pltpu.VMEM((2, t, 1), F32), pltpu.VMEM((2, t, 1), F32),
                        pltpu.VMEM((2, t, dv), F32)],
        compiler_params=_params("parallel", "arbitrary"),
        name="diff_attention",
    )(slopes, qh, kt, vh, lam_vecs.astype(F32), g_subln.reshape(1, dv).astype(F32))


def _ffn(h, g, w_in, w_out):
    a = rmsnorm(h, g, BF16)
    hid = swiglu_in(a, w_in.astype(BF16))
    return matmul_residual(hid, w_out.astype(BF16), h, 0.5)


def _fox_mixer(h, g, w_in, b_f, w_out, t):
    s, d = h.shape
    n_heads = d // HEAD_DIM
    nb = s // t
    hn = rmsnorm(h, g, BF16)
    qkv = matmul(hn, w_in[:, :3 * d].astype(BF16), BF16)
    w_f = jnp.zeros((d, V7X_LANES), BF16).at[:, :n_heads].set(w_in[:, 3 * d:].astype(BF16))
    b_pad = jnp.zeros((1, V7X_LANES), F32).at[0, :n_heads].set(b_f.astype(F32))
    nd = gate_cumsum(matmul(hn, w_f, F32), b_pad)
    nd = nd[:, :n_heads].T.reshape(n_heads, nb, 1, t)
    qh = qkv[:, :d].reshape(s, n_heads, HEAD_DIM).transpose(1, 0, 2)
    kt = qkv[:, d:2 * d].reshape(nb, t, n_heads, HEAD_DIM).transpose(2, 0, 3, 1)
    vh = qkv[:, 2 * d:].reshape(nb, t, n_heads, HEAD_DIM).transpose(2, 0, 1, 3)
    o = fox_attention(qh, kt, vh, nd, t)
    return matmul_residual(o, w_out.astype(BF16), h, 1.0)


def _diff_mixer(h, g, kt, vh, w_q, lam_vecs, g_subln, w_out, layer_idx, t):
    s, d = h.shape
    n_heads = d // (2 * HEAD_DIM)
    lam_init = 0.8 - 0.6 * math.exp(-0.3 * (layer_idx - 1))
    hn = rmsnorm(h, g, BF16)
    q = matmul(hn, w_q.astype(BF16), BF16)
    qh = q.reshape(s, n_heads, 2, HEAD_DIM).transpose(1, 2, 0, 3)
    o = diff_attention(qh, kt, vh, lam_vecs, g_subln, t, lam_init)
    return matmul_residual(o, w_out.astype(BF16), h, 1.0)


def kernel(x, g_ffn, w_ffn_in, w_ffn_out, g_mix, w_in_a, b_f, w_out_a, g_kv, w_kv, w_q_b, lam_b,
           g_subln, w_out_b, g_final):
    b, s, d = x.shape
    depth = g_ffn.shape[0]
    n_a = w_in_a.shape[0]
    t = _tile(s, 512)
    outs = []
    for bi in range(b):
        h = x[bi]
        kt = vh = None
        for l in range(depth):
            h = _ffn(h, g_ffn[l, 0], w_ffn_in[l, 0], w_ffn_out[l, 0])
            if l < n_a:
                h = _fox_mixer(h, g_mix[l], w_in_a[l], b_f[l], w_out_a[l], t)
            else:
                j = l - n_a
                h = _diff_mixer(h, g_mix[l], kt, vh, w_q_b[j], lam_b[j], g_subln[j], w_out_b[j],
                                l + 1, t)
            h = _ffn(h, g_ffn[l, 1], w_ffn_in[l, 1], w_ffn_out[l, 1])
            if l == n_a - 1:
                n_b = d // (2 * HEAD_DIM)
                nb = s // t
                kv = matmul(rmsnorm(h, g_kv, BF16), w_kv.astype(BF16), BF16)
                kt = kv[:, :d].reshape(nb, t, n_b, 2, HEAD_DIM).transpose(2, 3, 0, 4, 1)
                vh = kv[:, d:].reshape(nb, t, n_b, 2 * HEAD_DIM).transpose(2, 0, 1, 3)
        outs.append(rmsnorm(h, g_final, x.dtype))
    return jnp.stack(outs, axis=0)
```

```python
import functools
import math

import jax
import jax.numpy as jnp
from jax import lax
from jax.experimental import pallas as pl
from jax.experimental.pallas import tpu as pltpu

F32 = jnp.float32
BF16 = jnp.bfloat16

EPS = 1e-6
HEAD_DIM = 128
CHUNK = 64
LOG2E = 1.4426950408889634
NEG_INF = float("-inf")
QK_SCALE2 = (HEAD_DIM ** -0.5) * LOG2E

V7X_VMEM_LIMIT_BYTES = 56 * 1024 * 1024
V7X_LANES = 128
V7X_MXU_DIM = 256

ATT_TQ = 1024
ATT_TK = 512
ATT_CW = V7X_MXU_DIM
FOX_LOOKAHEAD = 3
DIFF_LOOKAHEAD = 2
FOX_AUG = 3


def _params(*sem):
    return pltpu.CompilerParams(dimension_semantics=sem, vmem_limit_bytes=V7X_VMEM_LIMIT_BYTES)


def _tile(dim, want):
    t = min(dim, want)
    while dim % t:
        t //= 2
    return t


def _rmsnorm_kernel(x_ref, g_ref, o_ref):
    x = x_ref[...]
    ms = jnp.mean(x * x, axis=-1, keepdims=True)
    o_ref[...] = (x * lax.rsqrt(ms + EPS) * g_ref[...]).astype(o_ref.dtype)


def rmsnorm(x, g, out_dtype, rows=512):
    s, d = x.shape
    tr = _tile(s, rows)
    return pl.pallas_call(
        _rmsnorm_kernel,
        out_shape=jax.ShapeDtypeStruct((s, d), out_dtype),
        grid=(s // tr,),
        in_specs=[pl.BlockSpec((tr, d), lambda i: (i, 0)),
                  pl.BlockSpec((1, d), lambda i: (0, 0))],
        out_specs=pl.BlockSpec((tr, d), lambda i: (i, 0)),
        compiler_params=_params("parallel"),
        name="rmsnorm",
    )(x, g.reshape(1, d).astype(F32))


def _mm_kernel(a_ref, w_ref, o_ref):
    o_ref[...] = jnp.dot(a_ref[...], w_ref[...], preferred_element_type=F32).astype(o_ref.dtype)


def _mm_scaled_kernel(a_ref, w_ref, s_ref, o_ref):
    acc = jnp.dot(a_ref[...], w_ref[...], preferred_element_type=F32)
    o_ref[...] = (acc * s_ref[...]).astype(o_ref.dtype)


def matmul(a, w, out_dtype, col_scale=None, tm=1024, tn=512):
    m, k = a.shape
    n = w.shape[1]
    tm, tn = _tile(m, tm), _tile(n, tn)
    in_specs = [pl.BlockSpec((tm, k), lambda i, j: (i, 0)),
                pl.BlockSpec((k, tn), lambda i, j: (0, j))]
    args = [a, w]
    body = _mm_kernel
    if col_scale is not None:
        in_specs.append(pl.BlockSpec((1, tn), lambda i, j: (0, j)))
        args.append(col_scale.reshape(1, n).astype(F32))
        body = _mm_scaled_kernel
    return pl.pallas_call(
        body,
        out_shape=jax.ShapeDtypeStruct((m, n), out_dtype),
        grid=(m // tm, n // tn),
        in_specs=in_specs,
        out_specs=pl.BlockSpec((tm, tn), lambda i, j: (i, j)),
        compiler_params=_params("parallel", "arbitrary"),
        name="matmul",
    )(*args)


def _mm_resid_kernel(a_ref, w_ref, r_ref, o_ref, *, alpha):
    acc = jnp.dot(a_ref[...], w_ref[...], preferred_element_type=F32)
    o_ref[...] = r_ref[...] + alpha * acc


def matmul_residual(a, w, resid, alpha, tm=1024, tn=512):
    m, k = a.shape
    n = w.shape[1]
    tm, tn = _tile(m, tm), _tile(n, tn)
    return pl.pallas_call(
        functools.partial(_mm_resid_kernel, alpha=alpha),
        out_shape=jax.ShapeDtypeStruct((m, n), F32),
        grid=(m // tm, n // tn),
        in_specs=[pl.BlockSpec((tm, k), lambda i, j: (i, 0)),
                  pl.BlockSpec((k, tn), lambda i, j: (0, j)),
                  pl.BlockSpec((tm, tn), lambda i, j: (i, j))],
        out_specs=pl.BlockSpec((tm, tn), lambda i, j: (i, j)),
        compiler_params=_params("parallel", "arbitrary"),
        name="matmul_residual",
    )(a, w, resid)


def _swiglu_kernel(a_ref, wg_ref, wu_ref, o_ref):
    a = a_ref[...]
    g = jnp.dot(a, wg_ref[...], preferred_element_type=F32)
    u = jnp.dot(a, wu_ref[...], preferred_element_type=F32)
    o_ref[...] = (g / (1.0 + jnp.exp(-g)) * u).astype(o_ref.dtype)


def swiglu_in(a, w_in, tm=1024, tn=512):
    m, k = a.shape
    f = w_in.shape[1] // 2
    tm, tn = _tile(m, tm), _tile(f, tn)
    nj = f // tn
    return pl.pallas_call(
        _swiglu_kernel,
        out_shape=jax.ShapeDtypeStruct((m, f), BF16),
        grid=(m // tm, nj),
        in_specs=[pl.BlockSpec((tm, k), lambda i, j: (i, 0)),
                  pl.BlockSpec((k, tn), lambda i, j: (0, j)),
                  pl.BlockSpec((k, tn), lambda i, j: (0, j + nj))],
        out_specs=pl.BlockSpec((tm, tn), lambda i, j: (i, j)),
        compiler_params=_params("parallel", "arbitrary"),
        name="swiglu_in",
    )(a, w_in, w_in)


def _split3(x):
    hi = x.astype(BF16)
    r1 = x - hi.astype(F32)
    mid = r1.astype(BF16)
    lo = (r1 - mid.astype(F32)).astype(BF16)
    return hi, mid, lo


def _gate_cumsum_kernel(f_ref, b_ref, o_ref, carry_ref):
    @pl.when(pl.program_id(0) == 0)
    def _():
        carry_ref[...] = jnp.zeros_like(carry_ref)

    x = f_ref[...] + b_ref[...]
    logf = jnp.minimum(x, 0.0) - jnp.log(1.0 + jnp.exp(-jnp.abs(x)))
    r = x.shape[0]
    row = lax.broadcasted_iota(jnp.int32, (r, r), 0)
    col = lax.broadcasted_iota(jnp.int32, (r, r), 1)
    tri = (col <= row).astype(BF16)
    hi, mid, lo = _split3(logf)
    csum = (jnp.dot(tri, lo, preferred_element_type=F32)
            + jnp.dot(tri, mid, preferred_element_type=F32)
            + jnp.dot(tri, hi, preferred_element_type=F32))
    tot = carry_ref[...] + csum
    carry_ref[...] = tot[r - 1:r, :]
    nd_hi, nd_mid, nd_lo = _split3(tot * (-LOG2E))
    o_ref[0] = nd_hi
    o_ref[1] = nd_mid
    o_ref[2] = nd_lo


def gate_cumsum(f, b, rows=256):
    s, w = f.shape
    tr = _tile(s, rows)
    return pl.pallas_call(
        _gate_cumsum_kernel,
        out_shape=jax.ShapeDtypeStruct((FOX_AUG, s, w), BF16),
        grid=(s // tr,),
        in_specs=[pl.BlockSpec((tr, w), lambda i: (i, 0)),
                  pl.BlockSpec((1, w), lambda i: (0, 0))],
        out_specs=pl.BlockSpec((FOX_AUG, tr, w), lambda i: (0, i, 0)),
        scratch_shapes=[pltpu.VMEM((1, w), F32)],
        compiler_params=_params("arbitrary"),
        name="gate_cumsum",
    )(f, b)


def _online_softmax_step(u, off, vt, m_ref, l_ref, acc_ref, idx):
    m_prev = m_ref[idx]
    m_new = jnp.maximum(m_prev, jnp.max(u, axis=0, keepdims=True) + off)
    alpha = jnp.exp2(m_prev - m_new)
    p = jnp.exp2(u - (m_new - off))
    l_ref[idx] = alpha * l_ref[idx] + jnp.sum(p, axis=0, keepdims=True)
    acc_ref[idx] = alpha * acc_ref[idx] + jnp.dot(vt, p.astype(BF16), preferred_element_type=F32)
    m_ref[idx] = m_new


def _diag_chunks(tq, tk, cw):
    return [(b, cc) for b in range(tq // tk) for cc in range(tq // cw)
            if (cc + 1) * cw - 1 >= b * tk]


def _body_chunks(tq, tk, cw):
    return [(b, cc) for b in range(tq // tk) for cc in range(tq // cw)]


def _prime_scores(first_steps, qk, u_ref):
    for n in range(u_ref.shape[0]):
        for mp, tile in enumerate(qk(first_steps[n])):
            u_ref[n, mp] = tile


def _pipelined_run(steps, next_steps, qk, fold, u_ref):
    n_steps = len(steps)
    lookahead = u_ref.shape[0]
    assert n_steps > lookahead
    pending = {}
    for n, step in enumerate(steps):
        ahead = n + lookahead
        if ahead < n_steps:
            pending[ahead] = qk(steps[ahead])
        elif next_steps is not None:
            for mp, tile in enumerate(qk(next_steps[ahead - n_steps])):
                u_ref[ahead - n_steps, mp] = tile
        fold(step, pending.pop(n) if n in pending
             else [u_ref[n, mp] for mp in range(u_ref.shape[1])])


def _fox_kernel(qt_ref, k_ref, vt_ref, o_ref, m_ref, l_ref, acc_ref, u_ref, *, tq, tk, cw):
    i = pl.program_id(1)
    bpq = tq // tk
    m_ref[...] = jnp.full_like(m_ref, NEG_INF)
    l_ref[...] = jnp.zeros_like(l_ref)
    acc_ref[...] = jnp.zeros_like(acc_ref)

    def qk(step):
        j, cc, _ = step
        return [jnp.dot(k_ref[j], qt_ref[:, cc * cw:(cc + 1) * cw], preferred_element_type=F32)]

    def fold(step, tiles):
        j, cc, key_rel0 = step
        u, = tiles
        if key_rel0 is not None and key_rel0 + tk - 1 > cc * cw:
            key = lax.broadcasted_iota(jnp.int32, (tk, cw), 0) + key_rel0
            qry = lax.broadcasted_iota(jnp.int32, (tk, cw), 1) + cc * cw
            u = jnp.where(key <= qry, u, NEG_INF)
        _online_softmax_step(u, 0.0, vt_ref[j], m_ref, l_ref, acc_ref,
                             (slice(None), slice(cc * cw, (cc + 1) * cw)))

    def full_steps(jj):
        return [(jj * bpq + b, cc, None) for b, cc in _body_chunks(tq, tk, cw)]

    diag_steps = [(i * bpq + b, cc, b * tk) for b, cc in _diag_chunks(tq, tk, cw)]
    lookahead = u_ref.shape[0]
    assert _diag_chunks(tq, tk, cw)[:lookahead] == _body_chunks(tq, tk, cw)[:lookahead]
    _prime_scores(full_steps(0), qk, u_ref)

    def body(jj, carry):
        _pipelined_run(full_steps(jj), full_steps(jj + 1), qk, fold, u_ref)
        return carry

    lax.fori_loop(0, i, body, 0)
    _pipelined_run(diag_steps, None, qk, fold, u_ref)
    o_ref[...] = (acc_ref[...] / l_ref[...]).T.astype(o_ref.dtype)


def fox_attention(qt, k, vt, tq, tk):
    h, kd, s = qt.shape
    dh = vt.shape[2]
    nkb = s // tk
    cw = min(ATT_CW, tq)
    return pl.pallas_call(
        functools.partial(_fox_kernel, tq=tq, tk=tk, cw=cw),
        out_shape=jax.ShapeDtypeStruct((s, h * dh), BF16),
        grid=(h, s // tq),
        in_specs=[pl.BlockSpec((None, kd, tq), lambda hh, i: (hh, 0, i)),
                  pl.BlockSpec((None, nkb, tk, kd), lambda hh, i: (hh, 0, 0, 0)),
                  pl.BlockSpec((None, nkb, dh, tk), lambda hh, i: (hh, 0, 0, 0))],
        out_specs=pl.BlockSpec((tq, dh), lambda hh, i: (i, hh)),
        scratch_shapes=[pltpu.VMEM((1, tq), F32), pltpu.VMEM((1, tq), F32),
                        pltpu.VMEM((dh, tq), F32), pltpu.VMEM((FOX_LOOKAHEAD, 1, tk, cw), F32)],
        compiler_params=_params("parallel", "arbitrary"),
        name="fox_attention",
    )(qt, k, vt)


def _diff_kernel(slope_ref, qt_ref, k_ref, vt_ref, lam_ref, g_ref, o_ref,
                 m_ref, l_ref, acc_ref, bt_ref, u_ref, *, tq, tk, cw, lam_init):
    hh = pl.program_id(0)
    i = pl.program_id(1)
    bpq = tq // tk
    beta = slope_ref[hh] * LOG2E
    m_ref[...] = jnp.full_like(m_ref, NEG_INF)
    l_ref[...] = jnp.zeros_like(l_ref)
    acc_ref[...] = jnp.zeros_like(acc_ref)
    key_i = lax.broadcasted_iota(jnp.int32, (tk, tq), 0)
    qry_i = lax.broadcasted_iota(jnp.int32, (tk, tq), 1)
    bt_ref[...] = (key_i - qry_i).astype(F32) * beta

    def qk(step):
        j, cc, _ = step
        lanes = slice(cc * cw, (cc + 1) * cw)
        return [jnp.dot(k_ref[mp, j], qt_ref[mp, :, lanes], preferred_element_type=F32)
                for mp in range(2)]

    def fold(step, u):
        j, cc, key_rel0 = step
        lanes = slice(cc * cw, (cc + 1) * cw)
        if key_rel0 is None:
            off = (j * tk - i * tq).astype(F32) * beta
            bias = bt_ref[:, lanes]
        else:
            off = 0.0
            key = lax.broadcasted_iota(jnp.int32, (tk, cw), 0) + key_rel0
            qry = lax.broadcasted_iota(jnp.int32, (tk, cw), 1) + cc * cw
            bias = jnp.abs(key - qry).astype(F32) * (-beta)
            bias = jnp.where(key // CHUNK <= qry // CHUNK, bias, NEG_INF)
        vt = vt_ref[j]
        for mp in range(2):
            _online_softmax_step(u[mp] + bias, off, vt, m_ref, l_ref, acc_ref,
                                 (mp, slice(None), lanes))

    def full_steps(jj):
        return [(jj * bpq + b, cc, None) for b, cc in _body_chunks(tq, tk, cw)]

    diag_steps = [(i * bpq + b, cc, b * tk) for b, cc in _diag_chunks(tq, tk, cw)]
    lookahead = u_ref.shape[0]
    assert _diag_chunks(tq, tk, cw)[:lookahead] == _body_chunks(tq, tk, cw)[:lookahead]
    _prime_scores(full_steps(0), qk, u_ref)

    def body(jj, carry):
        _pipelined_run(full_steps(jj), full_steps(jj + 1), qk, fold, u_ref)
        return carry

    lax.fori_loop(0, i, body, 0)
    _pipelined_run(diag_steps, None, qk, fold, u_ref)

    lv = lam_ref[...]
    lam = (jnp.exp(jnp.sum(lv[0:1] * lv[1:2], axis=-1, keepdims=True))
           - jnp.exp(jnp.sum(lv[2:3] * lv[3:4], axis=-1, keepdims=True)) + lam_init)
    o = acc_ref[0] / l_ref[0] - lam * (acc_ref[1] / l_ref[1])
    ms = jnp.mean(o * o, axis=0, keepdims=True)
    on = (o * lax.rsqrt(ms + EPS)).T
    o_ref[...] = (on * g_ref[...] * (1.0 - lam_init)).astype(o_ref.dtype)


def diff_attention(qt, k, vt, lam_vecs, g_subln, tq, tk, lam_init):
    h, _, dh, s = qt.shape
    dv = vt.shape[2]
    nkb = s // tk
    slopes = jnp.exp2(-8.0 * jnp.arange(1, h + 1, dtype=F32) / h)
    cw = min(ATT_CW, tq)
    return pl.pallas_call(
        functools.partial(_diff_kernel, tq=tq, tk=tk, cw=cw, lam_init=lam_init),
        out_shape=jax.ShapeDtypeStruct((s, h * dv), BF16),
        grid=(h, s // tq),
        in_specs=[pl.BlockSpec(memory_space=pltpu.SMEM),
                  pl.BlockSpec((None, 2, dh, tq), lambda hh, i: (hh, 0, 0, i)),
                  pl.BlockSpec((None, 2, nkb, tk, dh), lambda hh, i: (hh, 0, 0, 0, 0)),
                  pl.BlockSpec((None, nkb, dv, tk), lambda hh, i: (hh, 0, 0, 0)),
                  pl.BlockSpec((4, dh), lambda hh, i: (0, 0)),
                  pl.BlockSpec((1, dv), lambda hh, i: (0, 0))],
        out_specs=pl.BlockSpec((tq, dv), lambda hh, i: (i, hh)),
        scratch_shapes=[pltpu.VMEM((2, 1, tq), F32), pltpu.VMEM((2, 1, tq), F32),
                        pltpu.VMEM((2, dv, tq), F32), pltpu.VMEM((tk, tq), F32),
                        pltpu.VMEM((DIFF_LOOKAHEAD, 2, tk, cw), F32)],
        compiler_params=_params("parallel", "arbitrary"),
        name="diff_attention",
    )(slopes, qt, k, vt, lam_vecs.astype(F32), g_subln.reshape(1, dv).astype(F32))


def _ffn(h, g, w_in, w_out):
    a = rmsnorm(h, g, BF16)
    hid = swiglu_in(a, w_in.astype(BF16))
    return matmul_residual(hid, w_out.astype(BF16), h, 0.5)


def _fox_mixer(h, g, w_in, b_f, w_out, tq, tk):
    s, d = h.shape
    n_heads = d // HEAD_DIM
    nkb = s // tk
    hn = rmsnorm(h, g, BF16)
    scale = jnp.concatenate([jnp.full((d,), QK_SCALE2, F32), jnp.ones((2 * d,), F32)])
    qkv = matmul(hn, w_in[:, :3 * d].astype(BF16), BF16, col_scale=scale)
    w_f = jnp.zeros((d, V7X_LANES), BF16).at[:, :n_heads].set(w_in[:, 3 * d:].astype(BF16))
    b_pad = jnp.zeros((1, V7X_LANES), F32).at[0, :n_heads].set(b_f.astype(F32))
    nd3 = gate_cumsum(matmul(hn, w_f, F32), b_pad)[:, :, :n_heads]
    pad = V7X_MXU_DIM - HEAD_DIM - FOX_AUG
    qt = qkv[:, :d].reshape(s, n_heads, HEAD_DIM).transpose(1, 2, 0)
    qt = jnp.concatenate([qt, jnp.ones((n_heads, FOX_AUG, s), BF16),
                          jnp.zeros((n_heads, pad, s), BF16)], axis=1)
    k = qkv[:, d:2 * d].reshape(s, n_heads, HEAD_DIM).transpose(1, 0, 2)
    k = jnp.concatenate([k, nd3.transpose(2, 1, 0), jnp.zeros((n_heads, s, pad), BF16)], axis=2)
    k = k.reshape(n_heads, nkb, tk, V7X_MXU_DIM)
    vt = qkv[:, 2 * d:].reshape(nkb, tk, n_heads, HEAD_DIM).transpose(2, 0, 3, 1)
    o = fox_attention(qt, k, vt, tq, tk)
    return matmul_residual(o, w_out.astype(BF16), h, 1.0)


def _diff_mixer(h, g, k, vt, w_q, lam_vecs, g_subln, w_out, layer_idx, tq, tk):
    s, d = h.shape
    n_heads = d // (2 * HEAD_DIM)
    lam_init = 0.8 - 0.6 * math.exp(-0.3 * (layer_idx - 1))
    hn = rmsnorm(h, g, BF16)
    q = matmul(hn, w_q.astype(BF16), BF16, col_scale=jnp.full((d,), QK_SCALE2, F32))
    qt = q.reshape(s, n_heads, 2, HEAD_DIM).transpose(1, 2, 3, 0)
    o = diff_attention(qt, k, vt, lam_vecs, g_subln, tq, tk, lam_init)
    return matmul_residual(o, w_out.astype(BF16), h, 1.0)


def kernel(x, g_ffn, w_ffn_in, w_ffn_out, g_mix, w_in_a, b_f, w_out_a, g_kv, w_kv, w_q_b, lam_b,
           g_subln, w_out_b, g_final):
    b, s, d = x.shape
    depth = g_ffn.shape[0]
    n_a = w_in_a.shape[0]
    tq = _tile(s, ATT_TQ)
    tk = _tile(tq, ATT_TK)
    outs = []
    for bi in range(b):
        h = x[bi]
        k_sh = vt_sh = None
        for l in range(depth):
            h = _ffn(h, g_ffn[l, 0], w_ffn_in[l, 0], w_ffn_out[l, 0])
            if l < n_a:
                h = _fox_mixer(h, g_mix[l], w_in_a[l], b_f[l], w_out_a[l], tq, tk)
            else:
                j = l - n_a
                h = _diff_mixer(h, g_mix[l], k_sh, vt_sh, w_q_b[j], lam_b[j], g_subln[j],
                                w_out_b[j], l + 1, tq, tk)
            h = _ffn(h, g_ffn[l, 1], w_ffn_in[l, 1], w_ffn_out[l, 1])
            if l == n_a - 1:
                n_b = d // (2 * HEAD_DIM)
                nkb = s // tk
                kv = matmul(rmsnorm(h, g_kv, BF16), w_kv.astype(BF16), BF16)
                k_sh = kv[:, :d].reshape(nkb, tk, n_b, 2, HEAD_DIM).transpose(2, 3, 0, 1, 4)
                vt_sh = kv[:, d:].reshape(nkb, tk, n_b, 2 * HEAD_DIM).transpose(2, 0, 3, 1)
        outs.append(rmsnorm(h, g_final, x.dtype))
    return jnp.stack(outs, axis=0)
```

```python
import functools
import math

import jax
import jax.numpy as jnp
from jax import lax
from jax.experimental import pallas as pl
from jax.experimental.pallas import tpu as pltpu

F32 = jnp.float32
BF16 = jnp.bfloat16

EPS = 1e-6
HEAD_DIM = 128
CHUNK = 64
LOG2E = 1.4426950408889634
NEG_INF = float("-inf")
QK_SCALE2 = (HEAD_DIM ** -0.5) * LOG2E

V7X_VMEM_LIMIT_BYTES = 56 * 1024 * 1024
V7X_LANES = 128
V7X_MXU_DIM = 256
V7X_BF16_SUBLANES = 16

FOX_TQ = 2048
DIFF_TQ = 1024
ATT_TK = 512
ATT_CW = V7X_MXU_DIM
FOX_LOOKAHEAD = 3
DIFF_LOOKAHEAD = 2
FOX_AUG = 3


def _params(*sem):
    return pltpu.CompilerParams(dimension_semantics=sem, vmem_limit_bytes=V7X_VMEM_LIMIT_BYTES)


def _tile(dim, want):
    t = min(dim, want)
    while dim % t:
        t //= 2
    return t


def _rmsnorm_kernel(x_ref, g_ref, o_ref):
    x = x_ref[...]
    ms = jnp.mean(x * x, axis=-1, keepdims=True)
    o_ref[...] = (x * lax.rsqrt(ms + EPS) * g_ref[...]).astype(o_ref.dtype)


def rmsnorm(x, g, out_dtype, rows=512):
    s, d = x.shape
    tr = _tile(s, rows)
    return pl.pallas_call(
        _rmsnorm_kernel,
        out_shape=jax.ShapeDtypeStruct((s, d), out_dtype),
        grid=(s // tr,),
        in_specs=[pl.BlockSpec((tr, d), lambda i: (i, 0)),
                  pl.BlockSpec((1, d), lambda i: (0, 0))],
        out_specs=pl.BlockSpec((tr, d), lambda i: (i, 0)),
        compiler_params=_params("parallel"),
        name="rmsnorm",
    )(x, g.reshape(1, d).astype(F32))


def _mm_kernel(a_ref, w_ref, o_ref):
    o_ref[...] = jnp.dot(a_ref[...], w_ref[...], preferred_element_type=F32).astype(o_ref.dtype)


def _mm_scaled_kernel(a_ref, w_ref, s_ref, o_ref):
    acc = jnp.dot(a_ref[...], w_ref[...], preferred_element_type=F32)
    o_ref[...] = (acc * s_ref[...]).astype(o_ref.dtype)


def matmul(a, w, out_dtype, col_scale=None, tm=1024, tn=512):
    m, k = a.shape
    n = w.shape[1]
    tm, tn = _tile(m, tm), _tile(n, tn)
    in_specs = [pl.BlockSpec((tm, k), lambda i, j: (i, 0)),
                pl.BlockSpec((k, tn), lambda i, j: (0, j))]
    args = [a, w]
    body = _mm_kernel
    if col_scale is not None:
        in_specs.append(pl.BlockSpec((1, tn), lambda i, j: (0, j)))
        args.append(col_scale.reshape(1, n).astype(F32))
        body = _mm_scaled_kernel
    return pl.pallas_call(
        body,
        out_shape=jax.ShapeDtypeStruct((m, n), out_dtype),
        grid=(m // tm, n // tn),
        in_specs=in_specs,
        out_specs=pl.BlockSpec((tm, tn), lambda i, j: (i, j)),
        compiler_params=_params("parallel", "arbitrary"),
        name="matmul",
    )(*args)


def _mm_resid_kernel(a_ref, w_ref, r_ref, o_ref, *, alpha):
    acc = jnp.dot(a_ref[...], w_ref[...], preferred_element_type=F32)
    o_ref[...] = r_ref[...] + alpha * acc


def matmul_residual(a, w, resid, alpha, tm=1024, tn=512):
    m, k = a.shape
    n = w.shape[1]
    tm, tn = _tile(m, tm), _tile(n, tn)
    return pl.pallas_call(
        functools.partial(_mm_resid_kernel, alpha=alpha),
        out_shape=jax.ShapeDtypeStruct((m, n), F32),
        grid=(m // tm, n // tn),
        in_specs=[pl.BlockSpec((tm, k), lambda i, j: (i, 0)),
                  pl.BlockSpec((k, tn), lambda i, j: (0, j)),
                  pl.BlockSpec((tm, tn), lambda i, j: (i, j))],
        out_specs=pl.BlockSpec((tm, tn), lambda i, j: (i, j)),
        compiler_params=_params("parallel", "arbitrary"),
        name="matmul_residual",
    )(a, w, resid)


def _swiglu_kernel(a_ref, wg_ref, wu_ref, o_ref):
    a = a_ref[...]
    g = jnp.dot(a, wg_ref[...], preferred_element_type=F32)
    u = jnp.dot(a, wu_ref[...], preferred_element_type=F32)
    o_ref[...] = (g / (1.0 + jnp.exp(-g)) * u).astype(o_ref.dtype)


def swiglu_in(a, w_in, tm=1024, tn=512):
    m, k = a.shape
    f = w_in.shape[1] // 2
    tm, tn = _tile(m, tm), _tile(f, tn)
    nj = f // tn
    return pl.pallas_call(
        _swiglu_kernel,
        out_shape=jax.ShapeDtypeStruct((m, f), BF16),
        grid=(m // tm, nj),
        in_specs=[pl.BlockSpec((tm, k), lambda i, j: (i, 0)),
                  pl.BlockSpec((k, tn), lambda i, j: (0, j)),
                  pl.BlockSpec((k, tn), lambda i, j: (0, j + nj))],
        out_specs=pl.BlockSpec((tm, tn), lambda i, j: (i, j)),
        compiler_params=_params("parallel", "arbitrary"),
        name="swiglu_in",
    )(a, w_in, w_in)


def _split3(x):
    hi = x.astype(BF16)
    r1 = x - hi.astype(F32)
    mid = r1.astype(BF16)
    lo = (r1 - mid.astype(F32)).astype(BF16)
    return hi, mid, lo


def _gate_cumsum_kernel(f_ref, b_ref, o_ref, carry_ref):
    @pl.when(pl.program_id(0) == 0)
    def _():
        carry_ref[...] = jnp.zeros_like(carry_ref)

    x = f_ref[...] + b_ref[...]
    logf = jnp.minimum(x, 0.0) - jnp.log(1.0 + jnp.exp(-jnp.abs(x)))
    r = x.shape[0]
    row = lax.broadcasted_iota(jnp.int32, (r, r), 0)
    col = lax.broadcasted_iota(jnp.int32, (r, r), 1)
    tri = (col <= row).astype(BF16)
    hi, mid, lo = _split3(logf)
    csum = (jnp.dot(tri, lo, preferred_element_type=F32)
            + jnp.dot(tri, mid, preferred_element_type=F32)
            + jnp.dot(tri, hi, preferred_element_type=F32))
    tot = carry_ref[...] + csum
    carry_ref[...] = tot[r - 1:r, :]
    nd_hi, nd_mid, nd_lo = _split3(tot * (-LOG2E))
    o_ref[0] = nd_hi
    o_ref[1] = nd_mid
    o_ref[2] = nd_lo


def gate_cumsum(f, b, rows=256):
    s, w = f.shape
    tr = _tile(s, rows)
    return pl.pallas_call(
        _gate_cumsum_kernel,
        out_shape=jax.ShapeDtypeStruct((FOX_AUG, s, w), BF16),
        grid=(s // tr,),
        in_specs=[pl.BlockSpec((tr, w), lambda i: (i, 0)),
                  pl.BlockSpec((1, w), lambda i: (0, 0))],
        out_specs=pl.BlockSpec((FOX_AUG, tr, w), lambda i: (0, i, 0)),
        scratch_shapes=[pltpu.VMEM((1, w), F32)],
        compiler_params=_params("arbitrary"),
        name="gate_cumsum",
    )(f, b)


def _online_softmax_step(u, off, vt, m_ref, l_ref, acc_ref, idx):
    m_prev = m_ref[idx]
    m_new = jnp.maximum(m_prev, jnp.max(u, axis=0, keepdims=True) + off)
    alpha = jnp.exp2(m_prev - m_new)
    p = jnp.exp2(u - (m_new - off))
    if l_ref is not None:
        l_ref[idx] = alpha * l_ref[idx] + jnp.sum(p, axis=0, keepdims=True)
    acc_ref[idx] = alpha * acc_ref[idx] + jnp.dot(vt, p.astype(BF16), preferred_element_type=F32)
    m_ref[idx] = m_new


def _with_ones_row(vt):
    ones = jnp.ones(vt.shape[:-2] + (1, vt.shape[-1]), vt.dtype)
    zeros = jnp.zeros(vt.shape[:-2] + (V7X_BF16_SUBLANES - 1, vt.shape[-1]), vt.dtype)
    return jnp.concatenate([vt, ones, zeros], axis=-2)


def _diag_chunks(tq, tk, cw):
    return [(b, cc) for b in range(tq // tk) for cc in range(tq // cw)
            if (cc + 1) * cw - 1 >= b * tk]


def _body_chunks(tq, tk, cw):
    return [(b, cc) for b in range(tq // tk) for cc in range(tq // cw)]


def _prime_scores(first_steps, qk, u_ref):
    for n in range(u_ref.shape[0]):
        for mp, tile in enumerate(qk(first_steps[n])):
            u_ref[n, mp] = tile


def _pipelined_run(steps, next_steps, qk, fold, u_ref):
    n_steps = len(steps)
    lookahead = u_ref.shape[0]
    assert n_steps > lookahead
    pending = {}
    for n, step in enumerate(steps):
        ahead = n + lookahead
        if ahead < n_steps:
            pending[ahead] = qk(steps[ahead])
        elif next_steps is not None:
            for mp, tile in enumerate(qk(next_steps[ahead - n_steps])):
                u_ref[ahead - n_steps, mp] = tile
        fold(step, pending.pop(n) if n in pending
             else [u_ref[n, mp] for mp in range(u_ref.shape[1])])


def _fox_kernel(qt_ref, k_ref, vt_ref, o_ref, m_ref, acc_ref, u_ref, *, tq, tk, cw):
    i = pl.program_id(1)
    bpq = tq // tk
    dh = o_ref.shape[1]
    m_ref[...] = jnp.full_like(m_ref, NEG_INF)
    acc_ref[...] = jnp.zeros_like(acc_ref)

    def qk(step):
        j, cc, _ = step
        return [jnp.dot(k_ref[j], qt_ref[:, cc * cw:(cc + 1) * cw], preferred_element_type=F32)]

    def fold(step, tiles):
        j, cc, key_rel0 = step
        u, = tiles
        if key_rel0 is not None and key_rel0 + tk - 1 > cc * cw:
            key = lax.broadcasted_iota(jnp.int32, (tk, cw), 0) + key_rel0
            qry = lax.broadcasted_iota(jnp.int32, (tk, cw), 1) + cc * cw
            u = jnp.where(key <= qry, u, NEG_INF)
        _online_softmax_step(u, 0.0, vt_ref[j], m_ref, None, acc_ref,
                             (slice(None), slice(cc * cw, (cc + 1) * cw)))

    def full_steps(jj):
        return [(jj * bpq + b, cc, None) for b, cc in _body_chunks(tq, tk, cw)]

    diag_steps = [(i * bpq + b, cc, b * tk) for b, cc in _diag_chunks(tq, tk, cw)]
    lookahead = u_ref.shape[0]
    assert _diag_chunks(tq, tk, cw)[:lookahead] == _body_chunks(tq, tk, cw)[:lookahead]
    _prime_scores(full_steps(0), qk, u_ref)

    def body(jj, carry):
        _pipelined_run(full_steps(jj), full_steps(jj + 1), qk, fold, u_ref)
        return carry

    lax.fori_loop(0, i, body, 0)
    _pipelined_run(diag_steps, None, qk, fold, u_ref)
    o_ref[...] = (acc_ref[:dh, :] / acc_ref[dh:dh + 1, :]).T.astype(o_ref.dtype)


def fox_attention(qt, k, vt, tq, tk):
    h, kd, s = qt.shape
    dvp = vt.shape[2]
    dh = dvp - V7X_BF16_SUBLANES
    nkb = s // tk
    cw = min(ATT_CW, tq)
    once = pl.Buffered(1)
    return pl.pallas_call(
        functools.partial(_fox_kernel, tq=tq, tk=tk, cw=cw),
        out_shape=jax.ShapeDtypeStruct((s, h * dh), BF16),
        grid=(h, s // tq),
        in_specs=[pl.BlockSpec((None, kd, tq), lambda hh, i: (hh, 0, i)),
                  pl.BlockSpec((None, nkb, tk, kd), lambda hh, i: (hh, 0, 0, 0),
                               pipeline_mode=once),
                  pl.BlockSpec((None, nkb, dvp, tk), lambda hh, i: (hh, 0, 0, 0),
                               pipeline_mode=once)],
        out_specs=pl.BlockSpec((tq, dh), lambda hh, i: (i, hh)),
        scratch_shapes=[pltpu.VMEM((1, tq), F32), pltpu.VMEM((dvp, tq), F32),
                        pltpu.VMEM((FOX_LOOKAHEAD, 1, tk, cw), F32)],
        compiler_params=_params("parallel", "arbitrary"),
        name="fox_attention",
    )(qt, k, vt)


def _diff_kernel(slope_ref, qt_ref, k_ref, vt_ref, lam_ref, g_ref, o_ref,
                 m_ref, l_ref, acc_ref, bt_ref, u_ref, *, tq, tk, cw, lam_init):
    hh = pl.program_id(0)
    i = pl.program_id(1)
    bpq = tq // tk
    beta = slope_ref[hh] * LOG2E
    m_ref[...] = jnp.full_like(m_ref, NEG_INF)
    l_ref[...] = jnp.zeros_like(l_ref)
    acc_ref[...] = jnp.zeros_like(acc_ref)
    key_i = lax.broadcasted_iota(jnp.int32, (tk, tq), 0)
    qry_i = lax.broadcasted_iota(jnp.int32, (tk, tq), 1)
    bt_ref[...] = (key_i - qry_i).astype(F32) * beta

    def qk(step):
        j, cc, _ = step
        lanes = slice(cc * cw, (cc + 1) * cw)
        return [jnp.dot(k_ref[mp, j], qt_ref[mp, :, lanes], preferred_element_type=F32)
                for mp in range(2)]

    def fold(step, u):
        j, cc, key_rel0 = step
        lanes = slice(cc * cw, (cc + 1) * cw)
        if key_rel0 is None:
            off = (j * tk - i * tq).astype(F32) * beta
            bias = bt_ref[:, lanes]
        else:
            off = 0.0
            key = lax.broadcasted_iota(jnp.int32, (tk, cw), 0) + key_rel0
            qry = lax.broadcasted_iota(jnp.int32, (tk, cw), 1) + cc * cw
            bias = jnp.abs(key - qry).astype(F32) * (-beta)
            bias = jnp.where(key // CHUNK <= qry // CHUNK, bias, NEG_INF)
        vt = vt_ref[j]
        for mp in range(2):
            _online_softmax_step(u[mp] + bias, off, vt, m_ref, l_ref, acc_ref,
                                 (mp, slice(None), lanes))

    def full_steps(jj):
        return [(jj * bpq + b, cc, None) for b, cc in _body_chunks(tq, tk, cw)]

    diag_steps = [(i * bpq + b, cc, b * tk) for b, cc in _diag_chunks(tq, tk, cw)]
    lookahead = u_ref.shape[0]
    assert _diag_chunks(tq, tk, cw)[:lookahead] == _body_chunks(tq, tk, cw)[:lookahead]
    _prime_scores(full_steps(0), qk, u_ref)

    def body(jj, carry):
        _pipelined_run(full_steps(jj), full_steps(jj + 1), qk, fold, u_ref)
        return carry

    lax.fori_loop(0, i, body, 0)
    _pipelined_run(diag_steps, None, qk, fold, u_ref)

    lv = lam_ref[...]
    lam = (jnp.exp(jnp.sum(lv[0:1] * lv[1:2], axis=-1, keepdims=True))
           - jnp.exp(jnp.sum(lv[2:3] * lv[3:4], axis=-1, keepdims=True)) + lam_init)
    o = acc_ref[0] / l_ref[0] - lam * (acc_ref[1] / l_ref[1])
    ms = jnp.mean(o * o, axis=0, keepdims=True)
    on = (o * lax.rsqrt(ms + EPS)).T
    o_ref[...] = (on * g_ref[...] * (1.0 - lam_init)).astype(o_ref.dtype)


def diff_attention(qt, k, vt, lam_vecs, g_subln, tq, tk, lam_init):
    h, _, dh, s = qt.shape
    dv = vt.shape[2]
    nkb = s // tk
    slopes = jnp.exp2(-8.0 * jnp.arange(1, h + 1, dtype=F32) / h)
    cw = min(ATT_CW, tq)
    once = pl.Buffered(1)
    return pl.pallas_call(
        functools.partial(_diff_kernel, tq=tq, tk=tk, cw=cw, lam_init=lam_init),
        out_shape=jax.ShapeDtypeStruct((s, h * dv), BF16),
        grid=(h, s // tq),
        in_specs=[pl.BlockSpec(memory_space=pltpu.SMEM),
                  pl.BlockSpec((None, 2, dh, tq), lambda hh, i: (hh, 0, 0, i)),
                  pl.BlockSpec((None, 2, nkb, tk, dh), lambda hh, i: (hh, 0, 0, 0, 0),
                               pipeline_mode=once),
                  pl.BlockSpec((None, nkb, dv, tk), lambda hh, i: (hh, 0, 0, 0),
                               pipeline_mode=once),
                  pl.BlockSpec((4, dh), lambda hh, i: (0, 0)),
                  pl.BlockSpec((1, dv), lambda hh, i: (0, 0))],
        out_specs=pl.BlockSpec((tq, dv), lambda hh, i: (i, hh)),
        scratch_shapes=[pltpu.VMEM((2, 1, tq), F32), pltpu.VMEM((2, 1, tq), F32),
                        pltpu.VMEM((2, dv, tq), F32), pltpu.VMEM((tk, tq), F32),
                        pltpu.VMEM((DIFF_LOOKAHEAD, 2, tk, cw), F32)],
        compiler_params=_params("parallel", "arbitrary"),
        name="diff_attention",
    )(slopes, qt, k, vt, lam_vecs.astype(F32), g_subln.reshape(1, dv).astype(F32))


def _ffn(h, g, w_in, w_out):
    a = rmsnorm(h, g, BF16)
    hid = swiglu_in(a, w_in.astype(BF16))
    return matmul_residual(hid, w_out.astype(BF16), h, 0.5)


def _fox_mixer(h, g, w_in, b_f, w_out, tq, tk):
    s, d = h.shape
    n_heads = d // HEAD_DIM
    nkb = s // tk
    hn = rmsnorm(h, g, BF16)
    scale = jnp.concatenate([jnp.full((d,), QK_SCALE2, F32), jnp.ones((2 * d,), F32)])
    qkv = matmul(hn, w_in[:, :3 * d].astype(BF16), BF16, col_scale=scale)
    w_f = jnp.zeros((d, V7X_LANES), BF16).at[:, :n_heads].set(w_in[:, 3 * d:].astype(BF16))
    b_pad = jnp.zeros((1, V7X_LANES), F32).at[0, :n_heads].set(b_f.astype(F32))
    nd3 = gate_cumsum(matmul(hn, w_f, F32), b_pad)[:, :, :n_heads]
    pad = V7X_MXU_DIM - HEAD_DIM - FOX_AUG
    qt = qkv[:, :d].reshape(s, n_heads, HEAD_DIM).transpose(1, 2, 0)
    qt = jnp.concatenate([qt, jnp.ones((n_heads, FOX_AUG, s), BF16),
                          jnp.zeros((n_heads, pad, s), BF16)], axis=1)
    k = qkv[:, d:2 * d].reshape(s, n_heads, HEAD_DIM).transpose(1, 0, 2)
    k = jnp.concatenate([k, nd3.transpose(2, 1, 0), jnp.zeros((n_heads, s, pad), BF16)], axis=2)
    k = k.reshape(n_heads, nkb, tk, V7X_MXU_DIM)
    vt = qkv[:, 2 * d:].reshape(nkb, tk, n_heads, HEAD_DIM).transpose(2, 0, 3, 1)
    o = fox_attention(qt, k, _with_ones_row(vt), tq, tk)
    return matmul_residual(o, w_out.astype(BF16), h, 1.0)


def _diff_mixer(h, g, k, vt, w_q, lam_vecs, g_subln, w_out, layer_idx, tq, tk):
    s, d = h.shape
    n_heads = d // (2 * HEAD_DIM)
    lam_init = 0.8 - 0.6 * math.exp(-0.3 * (layer_idx - 1))
    hn = rmsnorm(h, g, BF16)
    q = matmul(hn, w_q.astype(BF16), BF16, col_scale=jnp.full((d,), QK_SCALE2, F32))
    qt = q.reshape(s, n_heads, 2, HEAD_DIM).transpose(1, 2, 3, 0)
    o = diff_attention(qt, k, vt, lam_vecs, g_subln, tq, tk, lam_init)
    return matmul_residual(o, w_out.astype(BF16), h, 1.0)


def kernel(x, g_ffn, w_ffn_in, w_ffn_out, g_mix, w_in_a, b_f, w_out_a, g_kv, w_kv, w_q_b, lam_b,
           g_subln, w_out_b, g_final):
    b, s, d = x.shape
    depth = g_ffn.shape[0]
    n_a = w_in_a.shape[0]
    tq_fox = _tile(s, FOX_TQ)
    tq_diff = _tile(s, DIFF_TQ)
    tk = _tile(min(tq_fox, tq_diff), ATT_TK)
    outs = []
    for bi in range(b):
        h = x[bi]
        k_sh = vt_sh = None
        for l in range(depth):
            h = _ffn(h, g_ffn[l, 0], w_ffn_in[l, 0], w_ffn_out[l, 0])
            if l < n_a:
                h = _fox_mixer(h, g_mix[l], w_in_a[l], b_f[l], w_out_a[l], tq_fox, tk)
            else:
                j = l - n_a
                h = _diff_mixer(h, g_mix[l], k_sh, vt_sh, w_q_b[j], lam_b[j], g_subln[j],
                                w_out_b[j], l + 1, tq_diff, tk)
            h = _ffn(h, g_ffn[l, 1], w_ffn_in[l, 1], w_ffn_out[l, 1])
            if l == n_a - 1:
                n_b = d // (2 * HEAD_DIM)
                nkb = s // tk
                kv = matmul(rmsnorm(h, g_kv, BF16), w_kv.astype(BF16), BF16)
                k_sh = kv[:, :d].reshape(nkb, tk, n_b, 2, HEAD_DIM).transpose(2, 3, 0, 1, 4)
                vt_sh = kv[:, d:].reshape(nkb, tk, n_b, 2 * HEAD_DIM).transpose(2, 0, 3, 1)
        outs.append(rmsnorm(h, g_final, x.dtype))
    return jnp.stack(outs, axis=0)
```

```python
import functools
import math

import jax
import jax.numpy as jnp
from jax import lax
from jax.experimental import pallas as pl
from jax.experimental.pallas import tpu as pltpu

F32 = jnp.float32
BF16 = jnp.bfloat16

EPS = 1e-6
HEAD_DIM = 128
CHUNK = 64
LOG2E = 1.4426950408889634
NEG_INF = float("-inf")
QK_SCALE2 = (HEAD_DIM ** -0.5) * LOG2E

V7X_VMEM_LIMIT_BYTES = 56 * 1024 * 1024
V7X_LANES = 128
V7X_MXU_DIM = 256
V7X_BF16_SUBLANES = 16

FOX_TQ = 2048
DIFF_TQ = 1024
ATT_TK = 512
ATT_CW = V7X_MXU_DIM
FOX_LOOKAHEAD = 3
DIFF_LOOKAHEAD = 2
FOX_AUG = 3


def _params(*sem):
    return pltpu.CompilerParams(dimension_semantics=sem, vmem_limit_bytes=V7X_VMEM_LIMIT_BYTES)


def _tile(dim, want):
    t = min(dim, want)
    while dim % t:
        t //= 2
    return t


def _rmsnorm_kernel(x_ref, g_ref, o_ref):
    x = x_ref[...]
    ms = jnp.mean(x * x, axis=-1, keepdims=True)
    o_ref[...] = (x * lax.rsqrt(ms + EPS) * g_ref[...]).astype(o_ref.dtype)


def rmsnorm(x, g, out_dtype, rows=512):
    s, d = x.shape
    tr = _tile(s, rows)
    return pl.pallas_call(
        _rmsnorm_kernel,
        out_shape=jax.ShapeDtypeStruct((s, d), out_dtype),
        grid=(s // tr,),
        in_specs=[pl.BlockSpec((tr, d), lambda i: (i, 0)),
                  pl.BlockSpec((1, d), lambda i: (0, 0))],
        out_specs=pl.BlockSpec((tr, d), lambda i: (i, 0)),
        compiler_params=_params("parallel"),
        name="rmsnorm",
    )(x, g.reshape(1, d).astype(F32))


def _mm_kernel(a_ref, w_ref, o_ref):
    o_ref[...] = jnp.dot(a_ref[...], w_ref[...], preferred_element_type=F32).astype(o_ref.dtype)


def _mm_scaled_kernel(a_ref, w_ref, s_ref, o_ref):
    acc = jnp.dot(a_ref[...], w_ref[...], preferred_element_type=F32)
    o_ref[...] = (acc * s_ref[...]).astype(o_ref.dtype)


def matmul(a, w, out_dtype, col_scale=None, tm=1024, tn=512):
    m, k = a.shape
    n = w.shape[1]
    tm, tn = _tile(m, tm), _tile(n, tn)
    in_specs = [pl.BlockSpec((tm, k), lambda i, j: (i, 0)),
                pl.BlockSpec((k, tn), lambda i, j: (0, j))]
    args = [a, w]
    body = _mm_kernel
    if col_scale is not None:
        in_specs.append(pl.BlockSpec((1, tn), lambda i, j: (0, j)))
        args.append(col_scale.reshape(1, n).astype(F32))
        body = _mm_scaled_kernel
    return pl.pallas_call(
        body,
        out_shape=jax.ShapeDtypeStruct((m, n), out_dtype),
        grid=(m // tm, n // tn),
        in_specs=in_specs,
        out_specs=pl.BlockSpec((tm, tn), lambda i, j: (i, j)),
        compiler_params=_params("parallel", "arbitrary"),
        name="matmul",
    )(*args)


def _proj_kernel(a_ref, w_ref, *rest, scale, layout, width, tk, pad_rows):
    o_ref = rest[-1]
    acc = jnp.dot(a_ref[...], w_ref[...], preferred_element_type=F32)
    if scale != 1.0:
        acc = acc * scale
    tm, tn = acc.shape
    if layout == "qt":
        o_ref[...] = acc.T.astype(o_ref.dtype)
        return
    res = acc.astype(o_ref.dtype)
    for hh in range(tn // width):
        for jb in range(tm // tk):
            if layout == "vt":
                blk = acc[jb * tk:(jb + 1) * tk, hh * width:(hh + 1) * width]
                o_ref[hh, jb, :width, :] = blk.T.astype(o_ref.dtype)
                if pad_rows:
                    row = lax.broadcasted_iota(jnp.int32, (pad_rows, tk), 0)
                    o_ref[hh, jb, width:, :] = (row == 0).astype(o_ref.dtype)
            else:
                o_ref[hh, jb, :, :width] = res[jb * tk:(jb + 1) * tk, hh * width:(hh + 1) * width]
                if len(rest) == 2:
                    o_ref[hh, jb, :, width:] = rest[0][hh, jb]


def project(a, w, layout, *, col0=0, ncols=None, scale=1.0, width=HEAD_DIM, tk=ATT_TK,
            pad_rows=0, aug=None, tm=1024, tn=512):
    m, k = a.shape
    n = w.shape[1] - col0 if ncols is None else ncols
    tm, tn = _tile(m, tm), _tile(n, tn)
    tk = _tile(tm, tk)
    assert col0 % tn == 0 and tn % width == 0
    jb0 = col0 // tn
    heads_t, blocks_t = tn // width, tm // tk
    in_specs = [pl.BlockSpec((tm, k), lambda i, j: (i, 0)),
                pl.BlockSpec((k, tn), lambda i, j: (0, j + jb0))]
    args = [a, w]
    if layout == "qt":
        out_shape = (n, m)
        out_spec = pl.BlockSpec((tn, tm), lambda i, j: (j, i))
    elif layout == "vt":
        out_shape = (n // width, m // tk, width + pad_rows, tk)
        out_spec = pl.BlockSpec((heads_t, blocks_t, width + pad_rows, tk), lambda i, j: (j, i, 0, 0))
    else:
        kd = width + (0 if aug is None else aug.shape[-1])
        out_shape = (n // width, m // tk, tk, kd)
        out_spec = pl.BlockSpec((heads_t, blocks_t, tk, kd), lambda i, j: (j, i, 0, 0))
        if aug is not None:
            in_specs.append(pl.BlockSpec((heads_t, blocks_t, tk, kd - width),
                                         lambda i, j: (j, i, 0, 0)))
            args.append(aug)
    return pl.pallas_call(
        functools.partial(_proj_kernel, scale=scale, layout=layout, width=width, tk=tk,
                          pad_rows=pad_rows),
        out_shape=jax.ShapeDtypeStruct(out_shape, BF16),
        grid=(m // tm, n // tn),
        in_specs=in_specs,
        out_specs=out_spec,
        compiler_params=_params("parallel", "arbitrary"),
        name="project_" + layout,
    )(*args)


def _mm_resid_kernel(a_ref, w_ref, r_ref, o_ref, *, alpha):
    acc = jnp.dot(a_ref[...], w_ref[...], preferred_element_type=F32)
    o_ref[...] = r_ref[...] + alpha * acc


def matmul_residual(a, w, resid, alpha, tm=1024, tn=512):
    m, k = a.shape
    n = w.shape[1]
    tm, tn = _tile(m, tm), _tile(n, tn)
    return pl.pallas_call(
        functools.partial(_mm_resid_kernel, alpha=alpha),
        out_shape=jax.ShapeDtypeStruct((m, n), F32),
        grid=(m // tm, n // tn),
        in_specs=[pl.BlockSpec((tm, k), lambda i, j: (i, 0)),
                  pl.BlockSpec((k, tn), lambda i, j: (0, j)),
                  pl.BlockSpec((tm, tn), lambda i, j: (i, j))],
        out_specs=pl.BlockSpec((tm, tn), lambda i, j: (i, j)),
        compiler_params=_params("parallel", "arbitrary"),
        name="matmul_residual",
    )(a, w, resid)


def _swiglu_kernel(a_ref, wg_ref, wu_ref, o_ref):
    a = a_ref[...]
    g = jnp.dot(a, wg_ref[...], preferred_element_type=F32)
    u = jnp.dot(a, wu_ref[...], preferred_element_type=F32)
    o_ref[...] = (g / (1.0 + jnp.exp(-g)) * u).astype(o_ref.dtype)


def swiglu_in(a, w_in, tm=1024, tn=512):
    m, k = a.shape
    f = w_in.shape[1] // 2
    tm, tn = _tile(m, tm), _tile(f, tn)
    nj = f // tn
    return pl.pallas_call(
        _swiglu_kernel,
        out_shape=jax.ShapeDtypeStruct((m, f), BF16),
        grid=(m // tm, nj),
        in_specs=[pl.BlockSpec((tm, k), lambda i, j: (i, 0)),
                  pl.BlockSpec((k, tn), lambda i, j: (0, j)),
                  pl.BlockSpec((k, tn), lambda i, j: (0, j + nj))],
        out_specs=pl.BlockSpec((tm, tn), lambda i, j: (i, j)),
        compiler_params=_params("parallel", "arbitrary"),
        name="swiglu_in",
    )(a, w_in, w_in)


def _split3(x):
    hi = x.astype(BF16)
    r1 = x - hi.astype(F32)
    mid = r1.astype(BF16)
    lo = (r1 - mid.astype(F32)).astype(BF16)
    return hi, mid, lo


def _gate_cumsum_kernel(f_ref, b_ref, o_ref, carry_ref):
    @pl.when(pl.program_id(0) == 0)
    def _():
        carry_ref[...] = jnp.zeros_like(carry_ref)

    x = f_ref[...] + b_ref[...]
    logf = jnp.minimum(x, 0.0) - jnp.log(1.0 + jnp.exp(-jnp.abs(x)))
    r = x.shape[0]
    row = lax.broadcasted_iota(jnp.int32, (r, r), 0)
    col = lax.broadcasted_iota(jnp.int32, (r, r), 1)
    tri = (col <= row).astype(BF16)
    hi, mid, lo = _split3(logf)
    csum = (jnp.dot(tri, lo, preferred_element_type=F32)
            + jnp.dot(tri, mid, preferred_element_type=F32)
            + jnp.dot(tri, hi, preferred_element_type=F32))
    tot = carry_ref[...] + csum
    carry_ref[...] = tot[r - 1:r, :]
    nd_hi, nd_mid, nd_lo = _split3(tot * (-LOG2E))
    o_ref[0] = nd_hi
    o_ref[1] = nd_mid
    o_ref[2] = nd_lo


def gate_cumsum(f, b, rows=256):
    s, w = f.shape
    tr = _tile(s, rows)
    return pl.pallas_call(
        _gate_cumsum_kernel,
        out_shape=jax.ShapeDtypeStruct((FOX_AUG, s, w), BF16),
        grid=(s // tr,),
        in_specs=[pl.BlockSpec((tr, w), lambda i: (i, 0)),
                  pl.BlockSpec((1, w), lambda i: (0, 0))],
        out_specs=pl.BlockSpec((FOX_AUG, tr, w), lambda i: (0, i, 0)),
        scratch_shapes=[pltpu.VMEM((1, w), F32)],
        compiler_params=_params("arbitrary"),
        name="gate_cumsum",
    )(f, b)


def _online_softmax_step(u, off, vt, m_ref, l_ref, acc_ref, idx):
    m_prev = m_ref[idx]
    m_new = jnp.maximum(m_prev, jnp.max(u, axis=0, keepdims=True) + off)
    alpha = jnp.exp2(m_prev - m_new)
    p = jnp.exp2(u - (m_new - off))
    if l_ref is not None:
        l_ref[idx] = alpha * l_ref[idx] + jnp.sum(p, axis=0, keepdims=True)
    acc_ref[idx] = alpha * acc_ref[idx] + jnp.dot(vt, p.astype(BF16), preferred_element_type=F32)
    m_ref[idx] = m_new


def _with_ones_row(vt):
    ones = jnp.ones(vt.shape[:-2] + (1, vt.shape[-1]), vt.dtype)
    zeros = jnp.zeros(vt.shape[:-2] + (V7X_BF16_SUBLANES - 1, vt.shape[-1]), vt.dtype)
    return jnp.concatenate([vt, ones, zeros], axis=-2)


def _diag_chunks(tq, tk, cw):
    return [(b, cc) for b in range(tq // tk) for cc in range(tq // cw)
            if (cc + 1) * cw - 1 >= b * tk]


def _body_chunks(tq, tk, cw):
    return [(b, cc) for b in range(tq // tk) for cc in range(tq // cw)]


def _prime_scores(first_steps, qk, u_ref):
    for n in range(u_ref.shape[0]):
        for mp, tile in enumerate(qk(first_steps[n])):
            u_ref[n, mp] = tile


def _pipelined_run(steps, next_steps, qk, fold, u_ref):
    n_steps = len(steps)
    lookahead = u_ref.shape[0]
    assert n_steps > lookahead
    pending = {}
    for n, step in enumerate(steps):
        ahead = n + lookahead
        if ahead < n_steps:
            pending[ahead] = qk(steps[ahead])
        elif next_steps is not None:
            for mp, tile in enumerate(qk(next_steps[ahead - n_steps])):
                u_ref[ahead - n_steps, mp] = tile
        fold(step, pending.pop(n) if n in pending
             else [u_ref[n, mp] for mp in range(u_ref.shape[1])])


def _fox_kernel(qt_ref, k_ref, vt_ref, o_ref, m_ref, acc_ref, u_ref, qa_ref, *, tq, tk, cw):
    i = pl.program_id(1)
    bpq = tq // tk
    dh = o_ref.shape[1]
    m_ref[...] = jnp.full_like(m_ref, NEG_INF)
    acc_ref[...] = jnp.zeros_like(acc_ref)
    qa_ref[:dh, :] = qt_ref[...]
    aug_row = lax.broadcasted_iota(jnp.int32, (qa_ref.shape[0] - dh, tq), 0)
    qa_ref[dh:, :] = (aug_row < FOX_AUG).astype(qa_ref.dtype)

    def qk(step):
        j, cc, _ = step
        return [jnp.dot(k_ref[j], qa_ref[:, cc * cw:(cc + 1) * cw], preferred_element_type=F32)]

    def fold(step, tiles):
        j, cc, key_rel0 = step
        u, = tiles
        if key_rel0 is not None and key_rel0 + tk - 1 > cc * cw:
            key = lax.broadcasted_iota(jnp.int32, (tk, cw), 0) + key_rel0
            qry = lax.broadcasted_iota(jnp.int32, (tk, cw), 1) + cc * cw
            u = jnp.where(key <= qry, u, NEG_INF)
        _online_softmax_step(u, 0.0, vt_ref[j], m_ref, None, acc_ref,
                             (slice(None), slice(cc * cw, (cc + 1) * cw)))

    def full_steps(jj):
        return [(jj * bpq + b, cc, None) for b, cc in _body_chunks(tq, tk, cw)]

    diag_steps = [(i * bpq + b, cc, b * tk) for b, cc in _diag_chunks(tq, tk, cw)]
    lookahead = u_ref.shape[0]
    assert _diag_chunks(tq, tk, cw)[:lookahead] == _body_chunks(tq, tk, cw)[:lookahead]
    _prime_scores(full_steps(0), qk, u_ref)

    def body(jj, carry):
        _pipelined_run(full_steps(jj), full_steps(jj + 1), qk, fold, u_ref)
        return carry

    lax.fori_loop(0, i, body, 0)
    _pipelined_run(diag_steps, None, qk, fold, u_ref)
    o_ref[...] = (acc_ref[:dh, :] / acc_ref[dh:dh + 1, :]).T.astype(o_ref.dtype)


def fox_attention(qt, k, vt, tq, tk):
    h, _, s = qt.shape
    kd = k.shape[-1]
    dvp = vt.shape[2]
    dh = dvp - V7X_BF16_SUBLANES
    nkb = s // tk
    cw = min(ATT_CW, tq)
    once = pl.Buffered(1)
    return pl.pallas_call(
        functools.partial(_fox_kernel, tq=tq, tk=tk, cw=cw),
        out_shape=jax.ShapeDtypeStruct((s, h * dh), BF16),
        grid=(h, s // tq),
        in_specs=[pl.BlockSpec((None, dh, tq), lambda hh, i: (hh, 0, i)),
                  pl.BlockSpec((None, nkb, tk, kd), lambda hh, i: (hh, 0, 0, 0),
                               pipeline_mode=once),
                  pl.BlockSpec((None, nkb, dvp, tk), lambda hh, i: (hh, 0, 0, 0),
                               pipeline_mode=once)],
        out_specs=pl.BlockSpec((tq, dh), lambda hh, i: (i, hh)),
        scratch_shapes=[pltpu.VMEM((1, tq), F32), pltpu.VMEM((dvp, tq), F32),
                        pltpu.VMEM((FOX_LOOKAHEAD, 1, tk, cw), F32),
                        pltpu.VMEM((kd, tq), BF16)],
        compiler_params=_params("parallel", "arbitrary"),
        name="fox_attention",
    )(qt, k, vt)


def _diff_kernel(slope_ref, qt_ref, k_ref, vt_ref, lam_ref, g_ref, o_ref,
                 m_ref, l_ref, acc_ref, bt_ref, u_ref, *, tq, tk, cw, lam_init):
    hh = pl.program_id(0)
    i = pl.program_id(1)
    bpq = tq // tk
    beta = slope_ref[hh] * LOG2E
    m_ref[...] = jnp.full_like(m_ref, NEG_INF)
    l_ref[...] = jnp.zeros_like(l_ref)
    acc_ref[...] = jnp.zeros_like(acc_ref)
    key_i = lax.broadcasted_iota(jnp.int32, (tk, tq), 0)
    qry_i = lax.broadcasted_iota(jnp.int32, (tk, tq), 1)
    bt_ref[...] = (key_i - qry_i).astype(F32) * beta

    def qk(step):
        j, cc, _ = step
        lanes = slice(cc * cw, (cc + 1) * cw)
        return [jnp.dot(k_ref[mp, j], qt_ref[mp, :, lanes], preferred_element_type=F32)
                for mp in range(2)]

    def fold(step, u):
        j, cc, key_rel0 = step
        lanes = slice(cc * cw, (cc + 1) * cw)
        if key_rel0 is None:
            off = (j * tk - i * tq).astype(F32) * beta
            bias = bt_ref[:, lanes]
        else:
            off = 0.0
            key = lax.broadcasted_iota(jnp.int32, (tk, cw), 0) + key_rel0
            qry = lax.broadcasted_iota(jnp.int32, (tk, cw), 1) + cc * cw
            bias = jnp.abs(key - qry).astype(F32) * (-beta)
            bias = jnp.where(key // CHUNK <= qry // CHUNK, bias, NEG_INF)
        vt = vt_ref[j]
        for mp in range(2):
            _online_softmax_step(u[mp] + bias, off, vt, m_ref, l_ref, acc_ref,
                                 (mp, slice(None), lanes))

    def full_steps(jj):
        return [(jj * bpq + b, cc, None) for b, cc in _body_chunks(tq, tk, cw)]

    diag_steps = [(i * bpq + b, cc, b * tk) for b, cc in _diag_chunks(tq, tk, cw)]
    lookahead = u_ref.shape[0]
    assert _diag_chunks(tq, tk, cw)[:lookahead] == _body_chunks(tq, tk, cw)[:lookahead]
    _prime_scores(full_steps(0), qk, u_ref)

    def body(jj, carry):
        _pipelined_run(full_steps(jj), full_steps(jj + 1), qk, fold, u_ref)
        return carry

    lax.fori_loop(0, i, body, 0)
    _pipelined_run(diag_steps, None, qk, fold, u_ref)

    lv = lam_ref[...]
    lam = (jnp.exp(jnp.sum(lv[0:1] * lv[1:2], axis=-1, keepdims=True))
           - jnp.exp(jnp.sum(lv[2:3] * lv[3:4], axis=-1, keepdims=True)) + lam_init)
    o = acc_ref[0] / l_ref[0] - lam * (acc_ref[1] / l_ref[1])
    ms = jnp.mean(o * o, axis=0, keepdims=True)
    on = (o * lax.rsqrt(ms + EPS)).T
    o_ref[...] = (on * g_ref[...] * (1.0 - lam_init)).astype(o_ref.dtype)


def diff_attention(qt, k, vt, lam_vecs, g_subln, tq, tk, lam_init):
    h, _, dh, s = qt.shape
    dv = vt.shape[2]
    nkb = s // tk
    slopes = jnp.exp2(-8.0 * jnp.arange(1, h + 1, dtype=F32) / h)
    cw = min(ATT_CW, tq)
    once = pl.Buffered(1)
    return pl.pallas_call(
        functools.partial(_diff_kernel, tq=tq, tk=tk, cw=cw, lam_init=lam_init),
        out_shape=jax.ShapeDtypeStruct((s, h * dv), BF16),
        grid=(h, s // tq),
        in_specs=[pl.BlockSpec(memory_space=pltpu.SMEM),
                  pl.BlockSpec((None, 2, dh, tq), lambda hh, i: (hh, 0, 0, i)),
                  pl.BlockSpec((None, 2, nkb, tk, dh), lambda hh, i: (hh, 0, 0, 0, 0),
                               pipeline_mode=once),
                  pl.BlockSpec((None, nkb, dv, tk), lambda hh, i: (hh, 0, 0, 0),
                               pipeline_mode=once),
                  pl.BlockSpec((4, dh), lambda hh, i: (0, 0)),
                  pl.BlockSpec((1, dv), lambda hh, i: (0, 0))],
        out_specs=pl.BlockSpec((tq, dv), lambda hh, i: (i, hh)),
        scratch_shapes=[pltpu.VMEM((2, 1, tq), F32), pltpu.VMEM((2, 1, tq), F32),
                        pltpu.VMEM((2, dv, tq), F32), pltpu.VMEM((tk, tq), F32),
                        pltpu.VMEM((DIFF_LOOKAHEAD, 2, tk, cw), F32)],
        compiler_params=_params("parallel", "arbitrary"),
        name="diff_attention",
    )(slopes, qt, k, vt, lam_vecs.astype(F32), g_subln.reshape(1, dv).astype(F32))


def _ffn(h, g, w_in, w_out):
    a = rmsnorm(h, g, BF16)
    hid = swiglu_in(a, w_in.astype(BF16))
    return matmul_residual(hid, w_out.astype(BF16), h, 0.5)


def _fox_mixer(h, g, w_in, b_f, w_out, tq, tk):
    s, d = h.shape
    n_heads = d // HEAD_DIM
    nkb = s // tk
    hn = rmsnorm(h, g, BF16)
    w = w_in.astype(BF16)
    w_f = jnp.zeros((d, V7X_LANES), BF16).at[:, :n_heads].set(w[:, 3 * d:])
    b_pad = jnp.zeros((1, V7X_LANES), F32).at[0, :n_heads].set(b_f.astype(F32))
    nd3 = gate_cumsum(matmul(hn, w_f, F32), b_pad)[:, :, :n_heads]
    aug = jnp.pad(nd3.transpose(2, 1, 0), ((0, 0), (0, 0), (0, V7X_MXU_DIM - HEAD_DIM - FOX_AUG)))
    aug = aug.reshape(n_heads, nkb, tk, V7X_MXU_DIM - HEAD_DIM)
    qt = project(hn, w, "qt", col0=0, ncols=d, scale=QK_SCALE2).reshape(n_heads, HEAD_DIM, s)
    k = project(hn, w, "k", col0=d, ncols=d, tk=tk, aug=aug)
    vt = project(hn, w, "vt", col0=2 * d, ncols=d, tk=tk, pad_rows=V7X_BF16_SUBLANES)
    o = fox_attention(qt, k, vt, tq, tk)
    return matmul_residual(o, w_out.astype(BF16), h, 1.0)


def _diff_mixer(h, g, k, vt, w_q, lam_vecs, g_subln, w_out, layer_idx, tq, tk):
    s, d = h.shape
    n_heads = d // (2 * HEAD_DIM)
    lam_init = 0.8 - 0.6 * math.exp(-0.3 * (layer_idx - 1))
    hn = rmsnorm(h, g, BF16)
    qt = project(hn, w_q.astype(BF16), "qt", scale=QK_SCALE2).reshape(n_heads, 2, HEAD_DIM, s)
    o = diff_attention(qt, k, vt, lam_vecs, g_subln, tq, tk, lam_init)
    return matmul_residual(o, w_out.astype(BF16), h, 1.0)


def kernel(x, g_ffn, w_ffn_in, w_ffn_out, g_mix, w_in_a, b_f, w_out_a, g_kv, w_kv, w_q_b, lam_b,
           g_subln, w_out_b, g_final):
    b, s, d = x.shape
    depth = g_ffn.shape[0]
    n_a = w_in_a.shape[0]
    tq_fox = _tile(s, FOX_TQ)
    tq_diff = _tile(s, DIFF_TQ)
    tk = _tile(min(tq_fox, tq_diff), ATT_TK)
    outs = []
    for bi in range(b):
        h = x[bi]
        k_sh = vt_sh = None
        for l in range(depth):
            h = _ffn(h, g_ffn[l, 0], w_ffn_in[l, 0], w_ffn_out[l, 0])
            if l < n_a:
                h = _fox_mixer(h, g_mix[l], w_in_a[l], b_f[l], w_out_a[l], tq_fox, tk)
            else:
                j = l - n_a
                h = _diff_mixer(h, g_mix[l], k_sh, vt_sh, w_q_b[j], lam_b[j], g_subln[j],
                                w_out_b[j], l + 1, tq_diff, tk)
            h = _ffn(h, g_ffn[l, 1], w_ffn_in[l, 1], w_ffn_out[l, 1])
            if l == n_a - 1:
                n_b = d // (2 * HEAD_DIM)
                nkb = s // tk
                kvn = rmsnorm(h, g_kv, BF16)
                wkv = w_kv.astype(BF16)
                k_sh = project(kvn, wkv, "k", col0=0, ncols=d, tk=tk)
                k_sh = k_sh.reshape(n_b, 2, nkb, tk, HEAD_DIM)
                vt_sh = project(kvn, wkv, "vt", col0=d, ncols=d, width=2 * HEAD_DIM, tk=tk)
        outs.append(rmsnorm(h, g_final, x.dtype))
    return jnp.stack(outs, axis=0)
```

```python
import functools
import math

import jax
import jax.numpy as jnp
from jax import lax
from jax.experimental import pallas as pl
from jax.experimental.pallas import tpu as pltpu

F32 = jnp.float32
BF16 = jnp.bfloat16

EPS = 1e-6
HEAD_DIM = 128
CHUNK = 64
LOG2E = 1.4426950408889634
NEG_INF = float("-inf")
QK_SCALE2 = (HEAD_DIM ** -0.5) * LOG2E

V7X_VMEM_LIMIT_BYTES = 56 * 1024 * 1024
V7X_LANES = 128
V7X_MXU_DIM = 256
V7X_BF16_SUBLANES = 16

FOX_TQ = 2048
DIFF_TQ = 1024
ATT_TK = 512
ATT_CW = V7X_MXU_DIM
FOX_LOOKAHEAD = 3
DIFF_LOOKAHEAD = 1
DIFF_ONES_PAD = 0
FOX_AUG = 3


def _params(*sem):
    return pltpu.CompilerParams(dimension_semantics=sem, vmem_limit_bytes=V7X_VMEM_LIMIT_BYTES)


def _tile(dim, want):
    t = min(dim, want)
    while dim % t:
        t //= 2
    return t


def _rmsnorm_kernel(x_ref, g_ref, o_ref):
    x = x_ref[...]
    ms = jnp.mean(x * x, axis=-1, keepdims=True)
    o_ref[...] = (x * lax.rsqrt(ms + EPS) * g_ref[...]).astype(o_ref.dtype)


def rmsnorm(x, g, out_dtype, rows=512):
    s, d = x.shape
    tr = _tile(s, rows)
    return pl.pallas_call(
        _rmsnorm_kernel,
        out_shape=jax.ShapeDtypeStruct((s, d), out_dtype),
        grid=(s // tr,),
        in_specs=[pl.BlockSpec((tr, d), lambda i: (i, 0)),
                  pl.BlockSpec((1, d), lambda i: (0, 0))],
        out_specs=pl.BlockSpec((tr, d), lambda i: (i, 0)),
        compiler_params=_params("parallel"),
        name="rmsnorm",
    )(x, g.reshape(1, d).astype(F32))


def _mm_kernel(a_ref, w_ref, o_ref):
    o_ref[...] = jnp.dot(a_ref[...], w_ref[...], preferred_element_type=F32).astype(o_ref.dtype)


def _mm_scaled_kernel(a_ref, w_ref, s_ref, o_ref):
    acc = jnp.dot(a_ref[...], w_ref[...], preferred_element_type=F32)
    o_ref[...] = (acc * s_ref[...]).astype(o_ref.dtype)


def matmul(a, w, out_dtype, col_scale=None, tm=1024, tn=512):
    m, k = a.shape
    n = w.shape[1]
    tm, tn = _tile(m, tm), _tile(n, tn)
    in_specs = [pl.BlockSpec((tm, k), lambda i, j: (i, 0)),
                pl.BlockSpec((k, tn), lambda i, j: (0, j))]
    args = [a, w]
    body = _mm_kernel
    if col_scale is not None:
        in_specs.append(pl.BlockSpec((1, tn), lambda i, j: (0, j)))
        args.append(col_scale.reshape(1, n).astype(F32))
        body = _mm_scaled_kernel
    return pl.pallas_call(
        body,
        out_shape=jax.ShapeDtypeStruct((m, n), out_dtype),
        grid=(m // tm, n // tn),
        in_specs=in_specs,
        out_specs=pl.BlockSpec((tm, tn), lambda i, j: (i, j)),
        compiler_params=_params("parallel", "arbitrary"),
        name="matmul",
    )(*args)


def _proj_kernel(a_ref, w_ref, *rest, scale, layout, width, tk, pad_rows):
    o_ref = rest[-1]
    acc = jnp.dot(a_ref[...], w_ref[...], preferred_element_type=F32)
    if scale != 1.0:
        acc = acc * scale
    tm, tn = acc.shape
    if layout == "qt":
        o_ref[...] = acc.T.astype(o_ref.dtype)
        return
    res = acc.astype(o_ref.dtype)
    for hh in range(tn // width):
        for jb in range(tm // tk):
            if layout == "vt":
                blk = acc[jb * tk:(jb + 1) * tk, hh * width:(hh + 1) * width]
                o_ref[hh, jb, :width, :] = blk.T.astype(o_ref.dtype)
                if pad_rows:
                    row = lax.broadcasted_iota(jnp.int32, (pad_rows, tk), 0)
                    o_ref[hh, jb, width:, :] = (row == 0).astype(o_ref.dtype)
            else:
                o_ref[hh, jb, :, :width] = res[jb * tk:(jb + 1) * tk, hh * width:(hh + 1) * width]
                if len(rest) == 2:
                    o_ref[hh, jb, :, width:] = rest[0][hh, jb]


def project(a, w, layout, *, lead=(), col0=0, ncols=None, scale=1.0, width=HEAD_DIM, tk=ATT_TK,
            pad_rows=0, aug=None, tm=1024, tn=512):
    m, k = a.shape
    n = w.shape[-1] - col0 if ncols is None else ncols
    tm, tn = _tile(m, tm), _tile(n, tn)
    tk = _tile(tm, tk)
    assert col0 % tn == 0 and tn % width == 0
    jb0 = col0 // tn
    heads_t, blocks_t = tn // width, tm // tk
    in_specs = [pl.BlockSpec((tm, k), lambda i, j: (i, 0)),
                _weight_spec(lead, k, tn, lambda j: j + jb0)]
    args = [a, w]
    if layout == "qt":
        out_shape = (n, m)
        out_spec = pl.BlockSpec((tn, tm), lambda i, j: (j, i))
    elif layout == "vt":
        out_shape = (n // width, m // tk, width + pad_rows, tk)
        out_spec = pl.BlockSpec((heads_t, blocks_t, width + pad_rows, tk), lambda i, j: (j, i, 0, 0))
    else:
        kd = width + (0 if aug is None else aug.shape[-1])
        out_shape = (n // width, m // tk, tk, kd)
        out_spec = pl.BlockSpec((heads_t, blocks_t, tk, kd), lambda i, j: (j, i, 0, 0))
        if aug is not None:
            in_specs.append(pl.BlockSpec((heads_t, blocks_t, tk, kd - width),
                                         lambda i, j: (j, i, 0, 0)))
            args.append(aug)
    return pl.pallas_call(
        functools.partial(_proj_kernel, scale=scale, layout=layout, width=width, tk=tk,
                          pad_rows=pad_rows),
        out_shape=jax.ShapeDtypeStruct(out_shape, BF16),
        grid=(m // tm, n // tn),
        in_specs=in_specs,
        out_specs=out_spec,
        compiler_params=_params("parallel", "arbitrary"),
        name="project_" + layout,
    )(*args)


def _mm_resid_kernel(a_ref, w_ref, r_ref, o_ref, *, alpha):
    acc = jnp.dot(a_ref[...], w_ref[...], preferred_element_type=F32)
    o_ref[...] = r_ref[...] + alpha * acc


def _weight_spec(lead, k, tn, col_block):
    lead = tuple(lead)
    return pl.BlockSpec((None,) * len(lead) + (k, tn),
                        lambda i, j: lead + (0, col_block(j)))


def matmul_residual(a, w, resid, alpha, lead=(), tm=1024, tn=512):
    m, k = a.shape
    n = w.shape[-1]
    tm, tn = _tile(m, tm), _tile(n, tn)
    return pl.pallas_call(
        functools.partial(_mm_resid_kernel, alpha=alpha),
        out_shape=jax.ShapeDtypeStruct((m, n), F32),
        grid=(m // tm, n // tn),
        in_specs=[pl.BlockSpec((tm, k), lambda i, j: (i, 0)),
                  _weight_spec(lead, k, tn, lambda j: j),
                  pl.BlockSpec((tm, tn), lambda i, j: (i, j))],
        out_specs=pl.BlockSpec((tm, tn), lambda i, j: (i, j)),
        compiler_params=_params("parallel", "arbitrary"),
        name="matmul_residual",
    )(a, w, resid)


def _swiglu_kernel(a_ref, wg_ref, wu_ref, o_ref):
    a = a_ref[...]
    g = jnp.dot(a, wg_ref[...], preferred_element_type=F32)
    u = jnp.dot(a, wu_ref[...], preferred_element_type=F32)
    o_ref[...] = (g / (1.0 + jnp.exp(-g)) * u).astype(o_ref.dtype)


def swiglu_in(a, w_in, lead=(), tm=1024, tn=512):
    m, k = a.shape
    f = w_in.shape[-1] // 2
    tm, tn = _tile(m, tm), _tile(f, tn)
    nj = f // tn
    return pl.pallas_call(
        _swiglu_kernel,
        out_shape=jax.ShapeDtypeStruct((m, f), BF16),
        grid=(m // tm, nj),
        in_specs=[pl.BlockSpec((tm, k), lambda i, j: (i, 0)),
                  _weight_spec(lead, k, tn, lambda j: j),
                  _weight_spec(lead, k, tn, lambda j: j + nj)],
        out_specs=pl.BlockSpec((tm, tn), lambda i, j: (i, j)),
        compiler_params=_params("parallel", "arbitrary"),
        name="swiglu_in",
    )(a, w_in, w_in)


def _split3(x):
    hi = x.astype(BF16)
    r1 = x - hi.astype(F32)
    mid = r1.astype(BF16)
    lo = (r1 - mid.astype(F32)).astype(BF16)
    return hi, mid, lo


def _gate_cumsum_kernel(f_ref, b_ref, o_ref, base_ref, carry_ref):
    @pl.when(pl.program_id(0) == 0)
    def _():
        carry_ref[...] = jnp.zeros_like(carry_ref)

    x = f_ref[...] + b_ref[...]
    logf = jnp.minimum(x, 0.0) - jnp.log(1.0 + jnp.exp(-jnp.abs(x)))
    r = x.shape[0]
    row = lax.broadcasted_iota(jnp.int32, (r, r), 0)
    col = lax.broadcasted_iota(jnp.int32, (r, r), 1)
    tri = (col <= row).astype(BF16)
    hi, mid, lo = _split3(logf)
    csum = (jnp.dot(tri, lo, preferred_element_type=F32)
            + jnp.dot(tri, mid, preferred_element_type=F32)
            + jnp.dot(tri, hi, preferred_element_type=F32))
    first = csum[0:1, :]
    base_ref[...] = (carry_ref[...] + first) * (-LOG2E)
    carry_ref[...] = carry_ref[...] + csum[r - 1:r, :]
    nd_hi, nd_mid, nd_lo = _split3((csum - first) * (-LOG2E))
    o_ref[0] = nd_hi
    o_ref[1] = nd_mid
    o_ref[2] = nd_lo


def gate_cumsum(f, b, rows):
    s, w = f.shape
    tr = rows
    return pl.pallas_call(
        _gate_cumsum_kernel,
        out_shape=(jax.ShapeDtypeStruct((FOX_AUG, s, w), BF16),
                   jax.ShapeDtypeStruct((s // tr, 1, w), F32)),
        grid=(s // tr,),
        in_specs=[pl.BlockSpec((tr, w), lambda i: (i, 0)),
                  pl.BlockSpec((1, w), lambda i: (0, 0))],
        out_specs=(pl.BlockSpec((FOX_AUG, tr, w), lambda i: (0, i, 0)),
                   pl.BlockSpec((None, 1, w), lambda i: (i, 0, 0))),
        scratch_shapes=[pltpu.VMEM((1, w), F32)],
        compiler_params=_params("arbitrary"),
        name="gate_cumsum",
    )(f, b)


def _online_softmax_step(u, off, vt, m_ref, l_ref, acc_ref, idx):
    m_prev = m_ref[idx]
    m_new = jnp.maximum(m_prev, jnp.max(u, axis=0, keepdims=True) + off)
    alpha = jnp.exp2(m_prev - m_new)
    p = jnp.exp2(u - (m_new - off))
    if l_ref is not None:
        l_ref[idx] = alpha * l_ref[idx] + jnp.sum(p, axis=0, keepdims=True)
    acc_ref[idx] = alpha * acc_ref[idx] + jnp.dot(vt, p.astype(BF16), preferred_element_type=F32)
    m_ref[idx] = m_new


def _with_ones_row(vt):
    ones = jnp.ones(vt.shape[:-2] + (1, vt.shape[-1]), vt.dtype)
    zeros = jnp.zeros(vt.shape[:-2] + (V7X_BF16_SUBLANES - 1, vt.shape[-1]), vt.dtype)
    return jnp.concatenate([vt, ones, zeros], axis=-2)


def _diag_chunks(tq, tk, cw):
    return [(b, cc) for b in range(tq // tk) for cc in range(tq // cw)
            if (cc + 1) * cw - 1 >= b * tk]


def _body_chunks(tq, tk, cw):
    return [(b, cc) for b in range(tq // tk) for cc in range(tq // cw)]


def _prime_scores(first_steps, qk, u_ref):
    for n in range(u_ref.shape[0] - 1):
        for mp, tile in enumerate(qk(first_steps[n])):
            u_ref[n, mp] = tile


def _pipelined_run(steps, next_steps, qk, fold, u_ref):
    n_steps = len(steps)
    ring = u_ref.shape[0]
    lookahead = ring - 1
    assert n_steps > lookahead and (next_steps is None or n_steps % ring == 0)
    for n, step in enumerate(steps):
        ahead = n + lookahead
        later = steps[ahead] if ahead < n_steps else (
            None if next_steps is None else next_steps[ahead - n_steps])
        if later is not None:
            for mp, tile in enumerate(qk(later)):
                u_ref[ahead % ring, mp] = tile
        fold(step, [u_ref[n % ring, mp] for mp in range(u_ref.shape[1])])


def _fox_kernel(base_ref, qt_ref, k_ref, vt_ref, o_ref, m_ref, acc_ref, u_ref, qa_ref,
                *, tq, tk, cw):
    hh = pl.program_id(0)
    i = pl.program_id(1)
    bpq = tq // tk
    dh = o_ref.shape[1]
    m_ref[...] = jnp.full_like(m_ref, NEG_INF)
    acc_ref[...] = jnp.zeros_like(acc_ref)
    qa_ref[:dh, :] = qt_ref[...]
    aug_row = lax.broadcasted_iota(jnp.int32, (qa_ref.shape[0] - dh, tq), 0)
    qa_ref[dh:, :] = (aug_row < FOX_AUG).astype(qa_ref.dtype)

    def qk(step):
        j, cc, _ = step
        return [jnp.dot(k_ref[j], qa_ref[:, cc * cw:(cc + 1) * cw], preferred_element_type=F32)]

    def fold(step, tiles):
        j, cc, key_rel0 = step
        u, = tiles
        if key_rel0 is not None and key_rel0 + tk - 1 > cc * cw:
            key = lax.broadcasted_iota(jnp.int32, (tk, cw), 0) + key_rel0
            qry = lax.broadcasted_iota(jnp.int32, (tk, cw), 1) + cc * cw
            u = jnp.where(key <= qry, u, NEG_INF)
        _online_softmax_step(u, base_ref[hh, j], vt_ref[j], m_ref, None, acc_ref,
                             (slice(None), slice(cc * cw, (cc + 1) * cw)))

    def full_steps(jj):
        return [(jj * bpq + b, cc, None) for b, cc in _body_chunks(tq, tk, cw)]

    diag_steps = [(i * bpq + b, cc, b * tk) for b, cc in _diag_chunks(tq, tk, cw)]
    lookahead = u_ref.shape[0] - 1
    assert _diag_chunks(tq, tk, cw)[:lookahead] == _body_chunks(tq, tk, cw)[:lookahead]
    _prime_scores(full_steps(0), qk, u_ref)

    def body(jj, carry):
        _pipelined_run(full_steps(jj), full_steps(jj + 1), qk, fold, u_ref)
        return carry

    lax.fori_loop(0, i, body, 0)
    _pipelined_run(diag_steps, None, qk, fold, u_ref)
    o_ref[...] = (acc_ref[:dh, :] / acc_ref[dh:dh + 1, :]).T.astype(o_ref.dtype)


def fox_attention(base, qt, k, vt, tq, tk):
    h, _, s = qt.shape
    kd = k.shape[-1]
    dvp = vt.shape[2]
    dh = dvp - V7X_BF16_SUBLANES
    nkb = s // tk
    cw = min(ATT_CW, tq)
    once = pl.Buffered(1)
    return pl.pallas_call(
        functools.partial(_fox_kernel, tq=tq, tk=tk, cw=cw),
        out_shape=jax.ShapeDtypeStruct((s, h * dh), BF16),
        grid=(h, s // tq),
        in_specs=[pl.BlockSpec(memory_space=pltpu.SMEM),
                  pl.BlockSpec((None, dh, tq), lambda hh, i: (hh, 0, i)),
                  pl.BlockSpec((None, nkb, tk, kd), lambda hh, i: (hh, 0, 0, 0),
                               pipeline_mode=once),
                  pl.BlockSpec((None, nkb, dvp, tk), lambda hh, i: (hh, 0, 0, 0),
                               pipeline_mode=once)],
        out_specs=pl.BlockSpec((tq, dh), lambda hh, i: (i, hh)),
        scratch_shapes=[pltpu.VMEM((1, tq), F32), pltpu.VMEM((dvp, tq), F32),
                        pltpu.VMEM((FOX_LOOKAHEAD + 1, 1, tk, cw), F32),
                        pltpu.VMEM((kd, tq), BF16)],
        compiler_params=_params("parallel", "arbitrary"),
        name="fox_attention",
    )(base, qt, k, vt)


def _diff_kernel(slope_ref, qt_ref, k_ref, vt_ref, lam_ref, g_ref, o_ref,
                 m_ref, l_ref, acc_ref, u_ref, bt_ref, *, tq, tk, cw, lam_init):
    hh = pl.program_id(0)
    i = pl.program_id(1)
    bpq = tq // tk
    beta = slope_ref[hh] * LOG2E
    m_ref[...] = jnp.full_like(m_ref, NEG_INF)
    l_ref[...] = jnp.zeros_like(l_ref)
    acc_ref[...] = jnp.zeros_like(acc_ref)
    dv = o_ref.shape[1]
    ones_row = acc_ref.shape[1] > dv
    key_i = lax.broadcasted_iota(jnp.int32, (tk, tq), 0)
    qry_i = lax.broadcasted_iota(jnp.int32, (tk, tq), 1)
    bt_ref[...] = (key_i - qry_i).astype(F32) * beta

    def qk(step):
        j, cc, _ = step
        lanes = slice(cc * cw, (cc + 1) * cw)
        return [jnp.dot(k_ref[mp, j], qt_ref[mp, :, lanes], preferred_element_type=F32)
                for mp in range(2)]

    def fold(step, u):
        j, cc, key_rel0 = step
        lanes = slice(cc * cw, (cc + 1) * cw)
        if key_rel0 is None:
            off = (j * tk - i * tq).astype(F32) * beta
            bias = bt_ref[:, lanes]
        else:
            off = 0.0
            key = lax.broadcasted_iota(jnp.int32, (tk, cw), 0) + key_rel0
            qry = lax.broadcasted_iota(jnp.int32, (tk, cw), 1) + cc * cw
            bias = jnp.abs(key - qry).astype(F32) * (-beta)
            bias = jnp.where(key // CHUNK <= qry // CHUNK, bias, NEG_INF)
        vt = vt_ref[j]
        for mp in range(2):
            _online_softmax_step(u[mp] + bias, off, vt, m_ref, None if ones_row else l_ref,
                                 acc_ref, (mp, slice(None), lanes))

    def full_steps(jj):
        return [(jj * bpq + b, cc, None) for b, cc in _body_chunks(tq, tk, cw)]

    diag_steps = [(i * bpq + b, cc, b * tk) for b, cc in _diag_chunks(tq, tk, cw)]
    lookahead = u_ref.shape[0] - 1
    assert _diag_chunks(tq, tk, cw)[:lookahead] == _body_chunks(tq, tk, cw)[:lookahead]
    _prime_scores(full_steps(0), qk, u_ref)

    def body(jj, carry):
        _pipelined_run(full_steps(jj), full_steps(jj + 1), qk, fold, u_ref)
        return carry

    lax.fori_loop(0, i, body, 0)
    _pipelined_run(diag_steps, None, qk, fold, u_ref)

    lv = lam_ref[...]
    lam = (jnp.exp(jnp.sum(lv[0:1] * lv[1:2], axis=-1, keepdims=True))
           - jnp.exp(jnp.sum(lv[2:3] * lv[3:4], axis=-1, keepdims=True)) + lam_init)
    den = [acc_ref[mp, dv:dv + 1, :] if ones_row else l_ref[mp] for mp in range(2)]
    o = acc_ref[0, :dv, :] / den[0] - lam * (acc_ref[1, :dv, :] / den[1])
    ms = jnp.mean(o * o, axis=0, keepdims=True)
    on = (o * lax.rsqrt(ms + EPS)).T
    o_ref[...] = (on * g_ref[...] * (1.0 - lam_init)).astype(o_ref.dtype)


def diff_attention(qt, k, vt, lam_vecs, g_subln, tq, tk, lam_init):
    h, _, dh, s = qt.shape
    kd = k.shape[-1]
    dvp = vt.shape[2]
    dv = g_subln.shape[-1]
    nkb = s // tk
    slopes = jnp.exp2(-8.0 * jnp.arange(1, h + 1, dtype=F32) / h)
    cw = min(ATT_CW, tq)
    once = pl.Buffered(1)
    return pl.pallas_call(
        functools.partial(_diff_kernel, tq=tq, tk=tk, cw=cw, lam_init=lam_init),
        out_shape=jax.ShapeDtypeStruct((s, h * dv), BF16),
        grid=(h, s // tq),
        in_specs=[pl.BlockSpec(memory_space=pltpu.SMEM),
                  pl.BlockSpec((None, 2, dh, tq), lambda hh, i: (hh, 0, 0, i)),
                  pl.BlockSpec((None, 2, nkb, tk, kd), lambda hh, i: (hh, 0, 0, 0, 0),
                               pipeline_mode=once),
                  pl.BlockSpec((None, nkb, dvp, tk), lambda hh, i: (hh, 0, 0, 0),
                               pipeline_mode=once),
                  pl.BlockSpec((4, dh), lambda hh, i: (0, 0)),
                  pl.BlockSpec((1, dv), lambda hh, i: (0, 0))],
        out_specs=pl.BlockSpec((tq, dv), lambda hh, i: (i, hh)),
        scratch_shapes=[pltpu.VMEM((2, 1, tq), F32), pltpu.VMEM((2, 1, tq), F32),
                        pltpu.VMEM((2, dvp, tq), F32),
                        pltpu.VMEM((DIFF_LOOKAHEAD + 1, 2, tk, cw), F32),
                        pltpu.VMEM((tk, tq), F32)],
        compiler_params=_params("parallel", "arbitrary"),
        name="diff_attention",
    )(slopes, qt, k, vt, lam_vecs.astype(F32), g_subln.reshape(1, dv).astype(F32))


def _ffn(h, g, w_in, w_out, lead):
    a = rmsnorm(h, g, BF16)
    hid = swiglu_in(a, w_in, lead)
    return matmul_residual(hid, w_out, h, 0.5, lead)


def _fox_mixer(h, g, w, b_f, w_out, l, tq, tk):
    s, d = h.shape
    n_heads = d // HEAD_DIM
    nkb = s // tk
    hn = rmsnorm(h, g, BF16)
    lead = (l,)
    w_f = jnp.zeros((d, V7X_LANES), BF16).at[:, :n_heads].set(w[l, :, 3 * d:])
    b_pad = jnp.zeros((1, V7X_LANES), F32).at[0, :n_heads].set(b_f.astype(F32))
    nd3, nd_base = gate_cumsum(matmul(hn, w_f, F32), b_pad, tk)
    nd3 = nd3[:, :, :n_heads]
    nd_base = nd_base[:, 0, :n_heads].T
    aug = jnp.pad(nd3.transpose(2, 1, 0), ((0, 0), (0, 0), (0, V7X_MXU_DIM - HEAD_DIM - FOX_AUG)))
    aug = aug.reshape(n_heads, nkb, tk, V7X_MXU_DIM - HEAD_DIM)
    qt = project(hn, w, "qt", lead=lead, col0=0, ncols=d, scale=QK_SCALE2)
    qt = qt.reshape(n_heads, HEAD_DIM, s)
    k = project(hn, w, "k", lead=lead, col0=d, ncols=d, tk=tk, aug=aug)
    vt = project(hn, w, "vt", lead=lead, col0=2 * d, ncols=d, tk=tk,
                 pad_rows=V7X_BF16_SUBLANES)
    o = fox_attention(nd_base, qt, k, vt, tq, tk)
    return matmul_residual(o, w_out, h, 1.0, lead)


def _diff_mixer(h, g, k, vt, w_q, lam_vecs, g_subln, w_out, j, layer_idx, tq, tk):
    s, d = h.shape
    n_heads = d // (2 * HEAD_DIM)
    lam_init = 0.8 - 0.6 * math.exp(-0.3 * (layer_idx - 1))
    hn = rmsnorm(h, g, BF16)
    qt = project(hn, w_q, "qt", lead=(j,), scale=QK_SCALE2).reshape(n_heads, 2, HEAD_DIM, s)
    o = diff_attention(qt, k, vt, lam_vecs, g_subln, tq, tk, lam_init)
    return matmul_residual(o, w_out, h, 1.0, (j,))


def kernel(x, g_ffn, w_ffn_in, w_ffn_out, g_mix, w_in_a, b_f, w_out_a, g_kv, w_kv, w_q_b, lam_b,
           g_subln, w_out_b, g_final):
    b, s, d = x.shape
    depth = g_ffn.shape[0]
    n_a = w_in_a.shape[0]
    tq_fox = _tile(s, FOX_TQ)
    tq_diff = _tile(s, DIFF_TQ)
    tk = _tile(min(tq_fox, tq_diff), ATT_TK)
    w_ffn_in, w_ffn_out, w_in_a, w_out_a, w_q_b, w_out_b = (
        w.astype(BF16) for w in (w_ffn_in, w_ffn_out, w_in_a, w_out_a, w_q_b, w_out_b))
    outs = []
    for bi in range(b):
        h = x[bi]
        k_sh = vt_sh = None
        for l in range(depth):
            h = _ffn(h, g_ffn[l, 0], w_ffn_in, w_ffn_out, (l, 0))
            if l < n_a:
                h = _fox_mixer(h, g_mix[l], w_in_a, b_f[l], w_out_a, l, tq_fox, tk)
            else:
                j = l - n_a
                h = _diff_mixer(h, g_mix[l], k_sh, vt_sh, w_q_b, lam_b[j], g_subln[j],
                                w_out_b, j, l + 1, tq_diff, tk)
            h = _ffn(h, g_ffn[l, 1], w_ffn_in, w_ffn_out, (l, 1))
            if l == n_a - 1:
                n_b = d // (2 * HEAD_DIM)
                nkb = s // tk
                kvn = rmsnorm(h, g_kv, BF16)
                wkv = w_kv.astype(BF16)
                k_sh = project(kvn, wkv, "k", col0=0, ncols=d, tk=tk)
                k_sh = k_sh.reshape(n_b, 2, nkb, tk, HEAD_DIM)
                vt_sh = project(kvn, wkv, "vt", col0=d, ncols=d, width=2 * HEAD_DIM, tk=tk,
                                pad_rows=DIFF_ONES_PAD)
        outs.append(rmsnorm(h, g_final, x.dtype))
    return jnp.stack(outs, axis=0)
```

```python
import functools
import math

import jax
import jax.numpy as jnp
from jax import lax
from jax.experimental import pallas as pl
from jax.experimental.pallas import tpu as pltpu

F32 = jnp.float32
BF16 = jnp.bfloat16

EPS = 1e-6
HEAD_DIM = 128
CHUNK = 64
LOG2E = 1.4426950408889634
NEG_INF = float("-inf")
QK_SCALE2 = (HEAD_DIM ** -0.5) * LOG2E

V7X_VMEM_LIMIT_BYTES = 56 * 1024 * 1024
V7X_LANES = 128
V7X_MXU_DIM = 256
V7X_BF16_SUBLANES = 16

FOX_TQ = 2048
DIFF_TQ = 1024
ATT_TK = 512
ATT_CW = V7X_MXU_DIM
FOX_LOOKAHEAD = 3
DIFF_LOOKAHEAD = 1
DIFF_ONES_PAD = 0
FOX_AUG = 3


def _params(*sem):
    return pltpu.CompilerParams(dimension_semantics=sem, vmem_limit_bytes=V7X_VMEM_LIMIT_BYTES)


def _tile(dim, want):
    t = min(dim, want)
    while dim % t:
        t //= 2
    return t


def _rmsnorm_kernel(x_ref, g_ref, o_ref):
    x = x_ref[...]
    ms = jnp.mean(x * x, axis=-1, keepdims=True)
    o_ref[...] = (x * lax.rsqrt(ms + EPS) * g_ref[...]).astype(o_ref.dtype)


def rmsnorm(x, g, out_dtype, rows=512):
    s, d = x.shape
    tr = _tile(s, rows)
    return pl.pallas_call(
        _rmsnorm_kernel,
        out_shape=jax.ShapeDtypeStruct((s, d), out_dtype),
        grid=(s // tr,),
        in_specs=[pl.BlockSpec((tr, d), lambda i: (i, 0)),
                  pl.BlockSpec((1, d), lambda i: (0, 0))],
        out_specs=pl.BlockSpec((tr, d), lambda i: (i, 0)),
        compiler_params=_params("parallel"),
        name="rmsnorm",
    )(x, g.reshape(1, d).astype(F32))


def _mm_kernel(a_ref, w_ref, o_ref):
    o_ref[...] = jnp.dot(a_ref[...], w_ref[...], preferred_element_type=F32).astype(o_ref.dtype)


def _mm_scaled_kernel(a_ref, w_ref, s_ref, o_ref):
    acc = jnp.dot(a_ref[...], w_ref[...], preferred_element_type=F32)
    o_ref[...] = (acc * s_ref[...]).astype(o_ref.dtype)


def matmul(a, w, out_dtype, col_scale=None, tm=1024, tn=512):
    m, k = a.shape
    n = w.shape[1]
    tm, tn = _tile(m, tm), _tile(n, tn)
    in_specs = [pl.BlockSpec((tm, k), lambda i, j: (i, 0)),
                pl.BlockSpec((k, tn), lambda i, j: (0, j))]
    args = [a, w]
    body = _mm_kernel
    if col_scale is not None:
        in_specs.append(pl.BlockSpec((1, tn), lambda i, j: (0, j)))
        args.append(col_scale.reshape(1, n).astype(F32))
        body = _mm_scaled_kernel
    return pl.pallas_call(
        body,
        out_shape=jax.ShapeDtypeStruct((m, n), out_dtype),
        grid=(m // tm, n // tn),
        in_specs=in_specs,
        out_specs=pl.BlockSpec((tm, tn), lambda i, j: (i, j)),
        compiler_params=_params("parallel", "arbitrary"),
        name="matmul",
    )(*args)


def _proj_kernel(a_ref, w_ref, *rest, scale, layout, width, tk, pad_rows):
    o_ref = rest[-1]
    acc = jnp.dot(a_ref[...], w_ref[...], preferred_element_type=F32)
    if scale != 1.0:
        acc = acc * scale
    tm, tn = acc.shape
    if layout == "qt":
        o_ref[...] = acc.T.astype(o_ref.dtype)
        return
    res = acc.astype(o_ref.dtype)
    for hh in range(tn // width):
        for jb in range(tm // tk):
            if layout == "vt":
                blk = acc[jb * tk:(jb + 1) * tk, hh * width:(hh + 1) * width]
                o_ref[hh, jb, :width, :] = blk.T.astype(o_ref.dtype)
                if pad_rows:
                    row = lax.broadcasted_iota(jnp.int32, (pad_rows, tk), 0)
                    o_ref[hh, jb, width:, :] = (row == 0).astype(o_ref.dtype)
            else:
                o_ref[hh, jb, :, :width] = res[jb * tk:(jb + 1) * tk, hh * width:(hh + 1) * width]
                if len(rest) == 2:
                    o_ref[hh, jb, :, width:] = rest[0][hh, jb]


def project(a, w, layout, *, lead=(), col0=0, ncols=None, scale=1.0, width=HEAD_DIM, tk=ATT_TK,
            pad_rows=0, aug=None, tm=1024, tn=512):
    m, k = a.shape
    n = w.shape[-1] - col0 if ncols is None else ncols
    tm, tn = _tile(m, tm), _tile(n, tn)
    tk = _tile(tm, tk)
    assert col0 % tn == 0 and tn % width == 0
    jb0 = col0 // tn
    heads_t, blocks_t = tn // width, tm // tk
    in_specs = [pl.BlockSpec((tm, k), lambda i, j: (i, 0)),
                _weight_spec(lead, k, tn, lambda j: j + jb0)]
    args = [a, w]
    if layout == "qt":
        out_shape = (n, m)
        out_spec = pl.BlockSpec((tn, tm), lambda i, j: (j, i))
    elif layout == "vt":
        out_shape = (n // width, m // tk, width + pad_rows, tk)
        out_spec = pl.BlockSpec((heads_t, blocks_t, width + pad_rows, tk), lambda i, j: (j, i, 0, 0))
    else:
        kd = width + (0 if aug is None else aug.shape[-1])
        out_shape = (n // width, m // tk, tk, kd)
        out_spec = pl.BlockSpec((heads_t, blocks_t, tk, kd), lambda i, j: (j, i, 0, 0))
        if aug is not None:
            in_specs.append(pl.BlockSpec((heads_t, blocks_t, tk, kd - width),
                                         lambda i, j: (j, i, 0, 0)))
            args.append(aug)
    return pl.pallas_call(
        functools.partial(_proj_kernel, scale=scale, layout=layout, width=width, tk=tk,
                          pad_rows=pad_rows),
        out_shape=jax.ShapeDtypeStruct(out_shape, BF16),
        grid=(m // tm, n // tn),
        in_specs=in_specs,
        out_specs=out_spec,
        compiler_params=_params("parallel", "arbitrary"),
        name="project_" + layout,
    )(*args)


def _mm_resid_kernel(a_ref, w_ref, r_ref, o_ref, *, alpha):
    acc = jnp.dot(a_ref[...], w_ref[...], preferred_element_type=F32)
    o_ref[...] = r_ref[...] + alpha * acc


def _weight_spec(lead, k, tn, col_block):
    lead = tuple(lead)
    return pl.BlockSpec((None,) * len(lead) + (k, tn),
                        lambda i, j: lead + (0, col_block(j)))


def matmul_residual(a, w, resid, alpha, lead=(), tm=1024, tn=512):
    m, k = a.shape
    n = w.shape[-1]
    tm, tn = _tile(m, tm), _tile(n, tn)
    return pl.pallas_call(
        functools.partial(_mm_resid_kernel, alpha=alpha),
        out_shape=jax.ShapeDtypeStruct((m, n), F32),
        grid=(m // tm, n // tn),
        in_specs=[pl.BlockSpec((tm, k), lambda i, j: (i, 0)),
                  _weight_spec(lead, k, tn, lambda j: j),
                  pl.BlockSpec((tm, tn), lambda i, j: (i, j))],
        out_specs=pl.BlockSpec((tm, tn), lambda i, j: (i, j)),
        compiler_params=_params("parallel", "arbitrary"),
        name="matmul_residual",
    )(a, w, resid)


def _swiglu_kernel(a_ref, wg_ref, wu_ref, o_ref):
    a = a_ref[...]
    g = jnp.dot(a, wg_ref[...], preferred_element_type=F32)
    u = jnp.dot(a, wu_ref[...], preferred_element_type=F32)
    o_ref[...] = (g / (1.0 + jnp.exp(-g)) * u).astype(o_ref.dtype)


def swiglu_in(a, w_in, lead=(), tm=1024, tn=512):
    m, k = a.shape
    f = w_in.shape[-1] // 2
    tm, tn = _tile(m, tm), _tile(f, tn)
    nj = f // tn
    return pl.pallas_call(
        _swiglu_kernel,
        out_shape=jax.ShapeDtypeStruct((m, f), BF16),
        grid=(m // tm, nj),
        in_specs=[pl.BlockSpec((tm, k), lambda i, j: (i, 0)),
                  _weight_spec(lead, k, tn, lambda j: j),
                  _weight_spec(lead, k, tn, lambda j: j + nj)],
        out_specs=pl.BlockSpec((tm, tn), lambda i, j: (i, j)),
        compiler_params=_params("parallel", "arbitrary"),
        name="swiglu_in",
    )(a, w_in, w_in)


def _split3(x):
    hi = x.astype(BF16)
    r1 = x - hi.astype(F32)
    mid = r1.astype(BF16)
    lo = (r1 - mid.astype(F32)).astype(BF16)
    return hi, mid, lo


def _gate_cumsum_kernel(f_ref, b_ref, o_ref, base_ref, carry_ref):
    @pl.when(pl.program_id(0) == 0)
    def _():
        carry_ref[...] = jnp.zeros_like(carry_ref)

    x = f_ref[...] + b_ref[...]
    logf = jnp.minimum(x, 0.0) - jnp.log(1.0 + jnp.exp(-jnp.abs(x)))
    r = x.shape[0]
    row = lax.broadcasted_iota(jnp.int32, (r, r), 0)
    col = lax.broadcasted_iota(jnp.int32, (r, r), 1)
    tri = (col <= row).astype(BF16)
    hi, mid, lo = _split3(logf)
    csum = (jnp.dot(tri, lo, preferred_element_type=F32)
            + jnp.dot(tri, mid, preferred_element_type=F32)
            + jnp.dot(tri, hi, preferred_element_type=F32))
    first = csum[0:1, :]
    base_ref[...] = (carry_ref[...] + first) * (-LOG2E)
    carry_ref[...] = carry_ref[...] + csum[r - 1:r, :]
    nd_hi, nd_mid, nd_lo = _split3((csum - first) * (-LOG2E))
    o_ref[0] = nd_hi
    o_ref[1] = nd_mid
    o_ref[2] = nd_lo


def gate_cumsum(f, b, rows):
    s, w = f.shape
    tr = rows
    return pl.pallas_call(
        _gate_cumsum_kernel,
        out_shape=(jax.ShapeDtypeStruct((FOX_AUG, s, w), BF16),
                   jax.ShapeDtypeStruct((s // tr, 1, w), F32)),
        grid=(s // tr,),
        in_specs=[pl.BlockSpec((tr, w), lambda i: (i, 0)),
                  pl.BlockSpec((1, w), lambda i: (0, 0))],
        out_specs=(pl.BlockSpec((FOX_AUG, tr, w), lambda i: (0, i, 0)),
                   pl.BlockSpec((None, 1, w), lambda i: (i, 0, 0))),
        scratch_shapes=[pltpu.VMEM((1, w), F32)],
        compiler_params=_params("arbitrary"),
        name="gate_cumsum",
    )(f, b)


def _online_softmax_step(u, off, vt, m_ref, l_ref, acc_ref, idx):
    m_prev = m_ref[idx]
    m_new = jnp.maximum(m_prev, jnp.max(u, axis=0, keepdims=True) + off)
    alpha = jnp.exp2(m_prev - m_new)
    p = jnp.exp2(u - (m_new - off))
    if l_ref is not None:
        l_ref[idx] = alpha * l_ref[idx] + jnp.sum(p, axis=0, keepdims=True)
    acc_ref[idx] = alpha * acc_ref[idx] + jnp.dot(vt, p.astype(BF16), preferred_element_type=F32)
    m_ref[idx] = m_new


def _with_ones_row(vt):
    ones = jnp.ones(vt.shape[:-2] + (1, vt.shape[-1]), vt.dtype)
    zeros = jnp.zeros(vt.shape[:-2] + (V7X_BF16_SUBLANES - 1, vt.shape[-1]), vt.dtype)
    return jnp.concatenate([vt, ones, zeros], axis=-2)


def _diag_chunks(tq, tk, cw):
    return [(b, cc) for b in range(tq // tk) for cc in range(tq // cw)
            if (cc + 1) * cw - 1 >= b * tk]


def _body_chunks(tq, tk, cw):
    return [(b, cc) for b in range(tq // tk) for cc in range(tq // cw)]


def _prime_scores(first_steps, qk, u_ref):
    for n in range(u_ref.shape[0] - 1):
        for mp, tile in enumerate(qk(first_steps[n])):
            u_ref[n, mp] = tile


def _pipelined_run(steps, next_steps, qk, fold, u_ref):
    n_steps = len(steps)
    ring = u_ref.shape[0]
    lookahead = ring - 1
    assert n_steps > lookahead and (next_steps is None or n_steps % ring == 0)
    for n, step in enumerate(steps):
        ahead = n + lookahead
        later = steps[ahead] if ahead < n_steps else (
            None if next_steps is None else next_steps[ahead - n_steps])
        if later is not None:
            for mp, tile in enumerate(qk(later)):
                u_ref[ahead % ring, mp] = tile
        fold(step, [u_ref[n % ring, mp] for mp in range(u_ref.shape[1])])


def _fox_kernel(base_ref, qt_ref, k_ref, vt_ref, o_ref, m_ref, acc_ref, u_ref, qa_ref,
                *, tq, tk, cw):
    hh = pl.program_id(0)
    i = pl.program_id(1)
    bpq = tq // tk
    dh = o_ref.shape[1]
    m_ref[...] = jnp.full_like(m_ref, NEG_INF)
    acc_ref[...] = jnp.zeros_like(acc_ref)
    qa_ref[:dh, :] = qt_ref[...]
    aug_row = lax.broadcasted_iota(jnp.int32, (qa_ref.shape[0] - dh, tq), 0)
    qa_ref[dh:, :] = (aug_row < FOX_AUG).astype(qa_ref.dtype)

    def qk(step):
        j, cc, _ = step
        return [jnp.dot(k_ref[j], qa_ref[:, cc * cw:(cc + 1) * cw], preferred_element_type=F32)]

    def fold(step, tiles):
        j, cc, key_rel0 = step
        u, = tiles
        if key_rel0 is not None and key_rel0 + tk - 1 > cc * cw:
            key = lax.broadcasted_iota(jnp.int32, (tk, cw), 0) + key_rel0
            qry = lax.broadcasted_iota(jnp.int32, (tk, cw), 1) + cc * cw
            u = jnp.where(key <= qry, u, NEG_INF)
        _online_softmax_step(u, base_ref[hh, j], vt_ref[j], m_ref, None, acc_ref,
                             (slice(None), slice(cc * cw, (cc + 1) * cw)))

    def full_steps(jj):
        return [(jj * bpq + b, cc, None) for b, cc in _body_chunks(tq, tk, cw)]

    diag_steps = [(i * bpq + b, cc, b * tk) for b, cc in _diag_chunks(tq, tk, cw)]
    lookahead = u_ref.shape[0] - 1
    assert _diag_chunks(tq, tk, cw)[:lookahead] == _body_chunks(tq, tk, cw)[:lookahead]
    _prime_scores(full_steps(0), qk, u_ref)

    def body(jj, carry):
        _pipelined_run(full_steps(jj), full_steps(jj + 1), qk, fold, u_ref)
        return carry

    lax.fori_loop(0, i, body, 0)
    _pipelined_run(diag_steps, None, qk, fold, u_ref)
    o_ref[...] = (acc_ref[:dh, :] / acc_ref[dh:dh + 1, :]).T.astype(o_ref.dtype)


def fox_attention(base, qt, k, vt, tq, tk):
    h, _, s = qt.shape
    kd = k.shape[-1]
    dvp = vt.shape[2]
    dh = dvp - V7X_BF16_SUBLANES
    nkb = s // tk
    cw = min(ATT_CW, tq)
    once = pl.Buffered(1)
    return pl.pallas_call(
        functools.partial(_fox_kernel, tq=tq, tk=tk, cw=cw),
        out_shape=jax.ShapeDtypeStruct((s, h * dh), BF16),
        grid=(h, s // tq),
        in_specs=[pl.BlockSpec(memory_space=pltpu.SMEM),
                  pl.BlockSpec((None, dh, tq), lambda hh, i: (hh, 0, i)),
                  pl.BlockSpec((None, nkb, tk, kd), lambda hh, i: (hh, 0, 0, 0),
                               pipeline_mode=once),
                  pl.BlockSpec((None, nkb, dvp, tk), lambda hh, i: (hh, 0, 0, 0),
                               pipeline_mode=once)],
        out_specs=pl.BlockSpec((tq, dh), lambda hh, i: (i, hh)),
        scratch_shapes=[pltpu.VMEM((1, tq), F32), pltpu.VMEM((dvp, tq), F32),
                        pltpu.VMEM((FOX_LOOKAHEAD + 1, 1, tk, cw), F32),
                        pltpu.VMEM((kd, tq), BF16)],
        compiler_params=_params("parallel", "arbitrary"),
        name="fox_attention",
    )(base, qt, k, vt)


def _diff_kernel(slope_ref, qt_ref, k_ref, vt_ref, lam_ref, g_ref, o_ref,
                 m_ref, l_ref, acc_ref, u_ref, bt_ref, bd_ref, *, tq, tk, cw, lam_init):
    hh = pl.program_id(0)
    i = pl.program_id(1)
    bpq = tq // tk
    beta = slope_ref[hh] * LOG2E
    m_ref[...] = jnp.full_like(m_ref, NEG_INF)
    l_ref[...] = jnp.zeros_like(l_ref)
    acc_ref[...] = jnp.zeros_like(acc_ref)
    dv = o_ref.shape[1]
    ones_row = acc_ref.shape[1] > dv
    diag_chunks = _diag_chunks(tq, tk, cw)

    @pl.when(i == 0)
    def _():
        key_i = lax.broadcasted_iota(jnp.int32, (tk, tq), 0)
        qry_i = lax.broadcasted_iota(jnp.int32, (tk, tq), 1)
        bt_ref[...] = (key_i - qry_i).astype(F32) * beta
        for n, (b, cc) in enumerate(diag_chunks):
            key = lax.broadcasted_iota(jnp.int32, (tk, cw), 0) + b * tk
            qry = lax.broadcasted_iota(jnp.int32, (tk, cw), 1) + cc * cw
            bias = jnp.abs(key - qry).astype(F32) * (-beta)
            bd_ref[n] = jnp.where(key // CHUNK <= qry // CHUNK, bias, NEG_INF)

    def qk(step):
        j, cc, _ = step
        lanes = slice(cc * cw, (cc + 1) * cw)
        return [jnp.dot(k_ref[mp, j], qt_ref[mp, :, lanes], preferred_element_type=F32)
                for mp in range(2)]

    def fold(step, u):
        j, cc, key_rel0 = step
        lanes = slice(cc * cw, (cc + 1) * cw)
        if key_rel0 is None:
            off = (j * tk - i * tq).astype(F32) * beta
            bias = bt_ref[:, lanes]
        else:
            off = 0.0
            bias = bd_ref[diag_chunks.index((key_rel0 // tk, cc))]
        vt = vt_ref[j]
        for mp in range(2):
            _online_softmax_step(u[mp] + bias, off, vt, m_ref, None if ones_row else l_ref,
                                 acc_ref, (mp, slice(None), lanes))

    def full_steps(jj):
        return [(jj * bpq + b, cc, None) for b, cc in _body_chunks(tq, tk, cw)]

    diag_steps = [(i * bpq + b, cc, b * tk) for b, cc in _diag_chunks(tq, tk, cw)]
    lookahead = u_ref.shape[0] - 1
    assert _diag_chunks(tq, tk, cw)[:lookahead] == _body_chunks(tq, tk, cw)[:lookahead]
    _prime_scores(full_steps(0), qk, u_ref)

    def body(jj, carry):
        _pipelined_run(full_steps(jj), full_steps(jj + 1), qk, fold, u_ref)
        return carry

    lax.fori_loop(0, i, body, 0)
    _pipelined_run(diag_steps, None, qk, fold, u_ref)

    lv = lam_ref[...]
    lam = (jnp.exp(jnp.sum(lv[0:1] * lv[1:2], axis=-1, keepdims=True))
           - jnp.exp(jnp.sum(lv[2:3] * lv[3:4], axis=-1, keepdims=True)) + lam_init)
    den = [acc_ref[mp, dv:dv + 1, :] if ones_row else l_ref[mp] for mp in range(2)]
    o = acc_ref[0, :dv, :] / den[0] - lam * (acc_ref[1, :dv, :] / den[1])
    ms = jnp.mean(o * o, axis=0, keepdims=True)
    on = (o * lax.rsqrt(ms + EPS)).T
    o_ref[...] = (on * g_ref[...] * (1.0 - lam_init)).astype(o_ref.dtype)


def diff_attention(qt, k, vt, lam_vecs, g_subln, tq, tk, lam_init):
    h, _, dh, s = qt.shape
    kd = k.shape[-1]
    dvp = vt.shape[2]
    dv = g_subln.shape[-1]
    nkb = s // tk
    slopes = jnp.exp2(-8.0 * jnp.arange(1, h + 1, dtype=F32) / h)
    cw = min(ATT_CW, tq)
    once = pl.Buffered(1)
    return pl.pallas_call(
        functools.partial(_diff_kernel, tq=tq, tk=tk, cw=cw, lam_init=lam_init),
        out_shape=jax.ShapeDtypeStruct((s, h * dv), BF16),
        grid=(h, s // tq),
        in_specs=[pl.BlockSpec(memory_space=pltpu.SMEM),
                  pl.BlockSpec((None, 2, dh, tq), lambda hh, i: (hh, 0, 0, i)),
                  pl.BlockSpec((None, 2, nkb, tk, kd), lambda hh, i: (hh, 0, 0, 0, 0),
                               pipeline_mode=once),
                  pl.BlockSpec((None, nkb, dvp, tk), lambda hh, i: (hh, 0, 0, 0),
                               pipeline_mode=once),
                  pl.BlockSpec((4, dh), lambda hh, i: (0, 0)),
                  pl.BlockSpec((1, dv), lambda hh, i: (0, 0))],
        out_specs=pl.BlockSpec((tq, dv), lambda hh, i: (i, hh)),
        scratch_shapes=[pltpu.VMEM((2, 1, tq), F32), pltpu.VMEM((2, 1, tq), F32),
                        pltpu.VMEM((2, dvp, tq), F32),
                        pltpu.VMEM((DIFF_LOOKAHEAD + 1, 2, tk, cw), F32),
                        pltpu.VMEM((tk, tq), F32),
                        pltpu.VMEM((len(_diag_chunks(tq, tk, cw)), tk, cw), F32)],
        compiler_params=_params("parallel", "arbitrary"),
        name="diff_attention",
    )(slopes, qt, k, vt, lam_vecs.astype(F32), g_subln.reshape(1, dv).astype(F32))


def _ffn(h, g, w_in, w_out, lead):
    a = rmsnorm(h, g, BF16)
    hid = swiglu_in(a, w_in, lead)
    return matmul_residual(hid, w_out, h, 0.5, lead)


def _fox_mixer(h, g, w, b_f, w_out, l, tq, tk):
    s, d = h.shape
    n_heads = d // HEAD_DIM
    nkb = s // tk
    hn = rmsnorm(h, g, BF16)
    lead = (l,)
    w_f = jnp.zeros((d, V7X_LANES), BF16).at[:, :n_heads].set(w[l, :, 3 * d:])
    b_pad = jnp.zeros((1, V7X_LANES), F32).at[0, :n_heads].set(b_f.astype(F32))
    nd3, nd_base = gate_cumsum(matmul(hn, w_f, F32), b_pad, tk)
    nd3 = nd3[:, :, :n_heads]
    nd_base = nd_base[:, 0, :n_heads].T
    aug = jnp.pad(nd3.transpose(2, 1, 0), ((0, 0), (0, 0), (0, V7X_MXU_DIM - HEAD_DIM - FOX_AUG)))
    aug = aug.reshape(n_heads, nkb, tk, V7X_MXU_DIM - HEAD_DIM)
    qt = project(hn, w, "qt", lead=lead, col0=0, ncols=d, scale=QK_SCALE2)
    qt = qt.reshape(n_heads, HEAD_DIM, s)
    k = project(hn, w, "k", lead=lead, col0=d, ncols=d, tk=tk, aug=aug)
    vt = project(hn, w, "vt", lead=lead, col0=2 * d, ncols=d, tk=tk,
                 pad_rows=V7X_BF16_SUBLANES)
    o = fox_attention(nd_base, qt, k, vt, tq, tk)
    return matmul_residual(o, w_out, h, 1.0, lead)


def _diff_mixer(h, g, k, vt, w_q, lam_vecs, g_subln, w_out, j, layer_idx, tq, tk):
    s, d = h.shape
    n_heads = d // (2 * HEAD_DIM)
    lam_init = 0.8 - 0.6 * math.exp(-0.3 * (layer_idx - 1))
    hn = rmsnorm(h, g, BF16)
    qt = project(hn, w_q, "qt", lead=(j,), scale=QK_SCALE2).reshape(n_heads, 2, HEAD_DIM, s)
    o = diff_attention(qt, k, vt, lam_vecs, g_subln, tq, tk, lam_init)
    return matmul_residual(o, w_out, h, 1.0, (j,))


def kernel(x, g_ffn, w_ffn_in, w_ffn_out, g_mix, w_in_a, b_f, w_out_a, g_kv, w_kv, w_q_b, lam_b,
           g_subln, w_out_b, g_final):
    b, s, d = x.shape
    depth = g_ffn.shape[0]
    n_a = w_in_a.shape[0]
    tq_fox = _tile(s, FOX_TQ)
    tq_diff = _tile(s, DIFF_TQ)
    tk = _tile(min(tq_fox, tq_diff), ATT_TK)
    w_ffn_in, w_ffn_out, w_in_a, w_out_a, w_q_b, w_out_b = (
        w.astype(BF16) for w in (w_ffn_in, w_ffn_out, w_in_a, w_out_a, w_q_b, w_out_b))
    outs = []
    for bi in range(b):
        h = x.reshape(s, d) if b == 1 else x[bi]
        k_sh = vt_sh = None
        for l in range(depth):
            h = _ffn(h, g_ffn[l, 0], w_ffn_in, w_ffn_out, (l, 0))
            if l < n_a:
                h = _fox_mixer(h, g_mix[l], w_in_a, b_f[l], w_out_a, l, tq_fox, tk)
            else:
                j = l - n_a
                h = _diff_mixer(h, g_mix[l], k_sh, vt_sh, w_q_b, lam_b[j], g_subln[j],
                                w_out_b, j, l + 1, tq_diff, tk)
            h = _ffn(h, g_ffn[l, 1], w_ffn_in, w_ffn_out, (l, 1))
            if l == n_a - 1:
                n_b = d // (2 * HEAD_DIM)
                nkb = s // tk
                kvn = rmsnorm(h, g_kv, BF16)
                wkv = w_kv.astype(BF16)
                k_sh = project(kvn, wkv, "k", col0=0, ncols=d, tk=tk)
                k_sh = k_sh.reshape(n_b, 2, nkb, tk, HEAD_DIM)
                vt_sh = project(kvn, wkv, "vt", col0=d, ncols=d, width=2 * HEAD_DIM, tk=tk,
                                pad_rows=DIFF_ONES_PAD)
        outs.append(rmsnorm(h, g_final, x.dtype))
    return outs[0].reshape(b, s, d) if b == 1 else jnp.stack(outs, axis=0)
```

```python
import functools
import math

import jax
import jax.numpy as jnp
from jax import lax
from jax.experimental import pallas as pl
from jax.experimental.pallas import tpu as pltpu

F32 = jnp.float32
BF16 = jnp.bfloat16

EPS = 1e-6
HEAD_DIM = 128
CHUNK = 64
LOG2E = 1.4426950408889634
NEG_INF = float("-inf")
QK_SCALE2 = (HEAD_DIM ** -0.5) * LOG2E

V7X_VMEM_LIMIT_BYTES = 56 * 1024 * 1024
V7X_LANES = 128
V7X_MXU_DIM = 256
V7X_BF16_SUBLANES = 16

FOX_TQ = 2048
DIFF_TQ = 2048
ATT_TK = 512
ATT_CW = V7X_MXU_DIM
FOX_LOOKAHEAD = 3
DIFF_LOOKAHEAD = 1
DIFF_ONES_PAD = 0
FOX_AUG = 3


def _params(*sem):
    return pltpu.CompilerParams(dimension_semantics=sem, vmem_limit_bytes=V7X_VMEM_LIMIT_BYTES)


def _tile(dim, want):
    t = min(dim, want)
    while dim % t:
        t //= 2
    return t


def _rmsnorm_kernel(x_ref, g_ref, o_ref):
    x = x_ref[...]
    ms = jnp.mean(x * x, axis=-1, keepdims=True)
    o_ref[...] = (x * lax.rsqrt(ms + EPS) * g_ref[...]).astype(o_ref.dtype)


def rmsnorm(x, g, out_dtype, rows=512):
    s, d = x.shape
    tr = _tile(s, rows)
    return pl.pallas_call(
        _rmsnorm_kernel,
        out_shape=jax.ShapeDtypeStruct((s, d), out_dtype),
        grid=(s // tr,),
        in_specs=[pl.BlockSpec((tr, d), lambda i: (i, 0)),
                  pl.BlockSpec((1, d), lambda i: (0, 0))],
        out_specs=pl.BlockSpec((tr, d), lambda i: (i, 0)),
        compiler_params=_params("parallel"),
        name="rmsnorm",
    )(x, g.reshape(1, d).astype(F32))


def _mm_kernel(a_ref, w_ref, o_ref):
    o_ref[...] = jnp.dot(a_ref[...], w_ref[...], preferred_element_type=F32).astype(o_ref.dtype)


def _mm_scaled_kernel(a_ref, w_ref, s_ref, o_ref):
    acc = jnp.dot(a_ref[...], w_ref[...], preferred_element_type=F32)
    o_ref[...] = (acc * s_ref[...]).astype(o_ref.dtype)


def matmul(a, w, out_dtype, col_scale=None, tm=1024, tn=512):
    m, k = a.shape
    n = w.shape[1]
    tm, tn = _tile(m, tm), _tile(n, tn)
    in_specs = [pl.BlockSpec((tm, k), lambda i, j: (i, 0)),
                pl.BlockSpec((k, tn), lambda i, j: (0, j))]
    args = [a, w]
    body = _mm_kernel
    if col_scale is not None:
        in_specs.append(pl.BlockSpec((1, tn), lambda i, j: (0, j)))
        args.append(col_scale.reshape(1, n).astype(F32))
        body = _mm_scaled_kernel
    return pl.pallas_call(
        body,
        out_shape=jax.ShapeDtypeStruct((m, n), out_dtype),
        grid=(m // tm, n // tn),
        in_specs=in_specs,
        out_specs=pl.BlockSpec((tm, tn), lambda i, j: (i, j)),
        compiler_params=_params("parallel", "arbitrary"),
        name="matmul",
    )(*args)


def _proj_kernel(a_ref, w_ref, *rest, scale, layout, width, tk, pad_rows):
    o_ref = rest[-1]
    acc = jnp.dot(a_ref[...], w_ref[...], preferred_element_type=F32)
    if scale != 1.0:
        acc = acc * scale
    tm, tn = acc.shape
    if layout == "qt":
        o_ref[...] = acc.T.astype(o_ref.dtype)
        return
    res = acc.astype(o_ref.dtype)
    for hh in range(tn // width):
        for jb in range(tm // tk):
            if layout == "vt":
                blk = acc[jb * tk:(jb + 1) * tk, hh * width:(hh + 1) * width]
                o_ref[hh, jb, :width, :] = blk.T.astype(o_ref.dtype)
                if pad_rows:
                    row = lax.broadcasted_iota(jnp.int32, (pad_rows, tk), 0)
                    o_ref[hh, jb, width:, :] = (row == 0).astype(o_ref.dtype)
            else:
                o_ref[hh, jb, :, :width] = res[jb * tk:(jb + 1) * tk, hh * width:(hh + 1) * width]
                if len(rest) == 2:
                    o_ref[hh, jb, :, width:] = rest[0][hh, jb]


def project(a, w, layout, *, lead=(), col0=0, ncols=None, scale=1.0, width=HEAD_DIM, tk=ATT_TK,
            pad_rows=0, aug=None, tm=1024, tn=512):
    m, k = a.shape
    n = w.shape[-1] - col0 if ncols is None else ncols
    tm, tn = _tile(m, tm), _tile(n, tn)
    tk = _tile(tm, tk)
    assert col0 % tn == 0 and tn % width == 0
    jb0 = col0 // tn
    heads_t, blocks_t = tn // width, tm // tk
    in_specs = [pl.BlockSpec((tm, k), lambda i, j: (i, 0)),
                _weight_spec(lead, k, tn, lambda j: j + jb0)]
    args = [a, w]
    if layout == "qt":
        out_shape = (n, m)
        out_spec = pl.BlockSpec((tn, tm), lambda i, j: (j, i))
    elif layout == "vt":
        out_shape = (n // width, m // tk, width + pad_rows, tk)
        out_spec = pl.BlockSpec((heads_t, blocks_t, width + pad_rows, tk), lambda i, j: (j, i, 0, 0))
    else:
        kd = width + (0 if aug is None else aug.shape[-1])
        out_shape = (n // width, m // tk, tk, kd)
        out_spec = pl.BlockSpec((heads_t, blocks_t, tk, kd), lambda i, j: (j, i, 0, 0))
        if aug is not None:
            in_specs.append(pl.BlockSpec((heads_t, blocks_t, tk, kd - width),
                                         lambda i, j: (j, i, 0, 0)))
            args.append(aug)
    return pl.pallas_call(
        functools.partial(_proj_kernel, scale=scale, layout=layout, width=width, tk=tk,
                          pad_rows=pad_rows),
        out_shape=jax.ShapeDtypeStruct(out_shape, BF16),
        grid=(m // tm, n // tn),
        in_specs=in_specs,
        out_specs=out_spec,
        compiler_params=_params("parallel", "arbitrary"),
        name="project_" + layout,
    )(*args)


def _mm_resid_kernel(a_ref, w_ref, r_ref, o_ref, *, alpha):
    acc = jnp.dot(a_ref[...], w_ref[...], preferred_element_type=F32)
    o_ref[...] = r_ref[...] + alpha * acc


def _weight_spec(lead, k, tn, col_block):
    lead = tuple(lead)
    return pl.BlockSpec((None,) * len(lead) + (k, tn),
                        lambda i, j: lead + (0, col_block(j)))


def matmul_residual(a, w, resid, alpha, lead=(), tm=1024, tn=512):
    m, k = a.shape
    n = w.shape[-1]
    tm, tn = _tile(m, tm), _tile(n, tn)
    return pl.pallas_call(
        functools.partial(_mm_resid_kernel, alpha=alpha),
        out_shape=jax.ShapeDtypeStruct((m, n), F32),
        grid=(m // tm, n // tn),
        in_specs=[pl.BlockSpec((tm, k), lambda i, j: (i, 0)),
                  _weight_spec(lead, k, tn, lambda j: j),
                  pl.BlockSpec((tm, tn), lambda i, j: (i, j))],
        out_specs=pl.BlockSpec((tm, tn), lambda i, j: (i, j)),
        compiler_params=_params("parallel", "arbitrary"),
        name="matmul_residual",
    )(a, w, resid)


def _swiglu_kernel(a_ref, wg_ref, wu_ref, o_ref):
    a = a_ref[...]
    g = jnp.dot(a, wg_ref[...], preferred_element_type=F32)
    u = jnp.dot(a, wu_ref[...], preferred_element_type=F32)
    o_ref[...] = (g / (1.0 + jnp.exp(-g)) * u).astype(o_ref.dtype)


def swiglu_in(a, w_in, lead=(), tm=1024, tn=512):
    m, k = a.shape
    f = w_in.shape[-1] // 2
    tm, tn = _tile(m, tm), _tile(f, tn)
    nj = f // tn
    return pl.pallas_call(
        _swiglu_kernel,
        out_shape=jax.ShapeDtypeStruct((m, f), BF16),
        grid=(m // tm, nj),
        in_specs=[pl.BlockSpec((tm, k), lambda i, j: (i, 0)),
                  _weight_spec(lead, k, tn, lambda j: j),
                  _weight_spec(lead, k, tn, lambda j: j + nj)],
        out_specs=pl.BlockSpec((tm, tn), lambda i, j: (i, j)),
        compiler_params=_params("parallel", "arbitrary"),
        name="swiglu_in",
    )(a, w_in, w_in)


def _split3(x):
    hi = x.astype(BF16)
    r1 = x - hi.astype(F32)
    mid = r1.astype(BF16)
    lo = (r1 - mid.astype(F32)).astype(BF16)
    return hi, mid, lo


def _gate_cumsum_kernel(f_ref, b_ref, o_ref, base_ref, carry_ref):
    @pl.when(pl.program_id(0) == 0)
    def _():
        carry_ref[...] = jnp.zeros_like(carry_ref)

    x = f_ref[...] + b_ref[...]
    logf = jnp.minimum(x, 0.0) - jnp.log(1.0 + jnp.exp(-jnp.abs(x)))
    r = x.shape[0]
    row = lax.broadcasted_iota(jnp.int32, (r, r), 0)
    col = lax.broadcasted_iota(jnp.int32, (r, r), 1)
    tri = (col <= row).astype(BF16)
    hi, mid, lo = _split3(logf)
    csum = (jnp.dot(tri, lo, preferred_element_type=F32)
            + jnp.dot(tri, mid, preferred_element_type=F32)
            + jnp.dot(tri, hi, preferred_element_type=F32))
    first = csum[0:1, :]
    base_ref[...] = (carry_ref[...] + first) * (-LOG2E)
    carry_ref[...] = carry_ref[...] + csum[r - 1:r, :]
    nd_hi, nd_mid, nd_lo = _split3((csum - first) * (-LOG2E))
    o_ref[0] = nd_hi
    o_ref[1] = nd_mid
    o_ref[2] = nd_lo


def gate_cumsum(f, b, rows):
    s, w = f.shape
    tr = rows
    return pl.pallas_call(
        _gate_cumsum_kernel,
        out_shape=(jax.ShapeDtypeStruct((FOX_AUG, s, w), BF16),
                   jax.ShapeDtypeStruct((s // tr, 1, w), F32)),
        grid=(s // tr,),
        in_specs=[pl.BlockSpec((tr, w), lambda i: (i, 0)),
                  pl.BlockSpec((1, w), lambda i: (0, 0))],
        out_specs=(pl.BlockSpec((FOX_AUG, tr, w), lambda i: (0, i, 0)),
                   pl.BlockSpec((None, 1, w), lambda i: (i, 0, 0))),
        scratch_shapes=[pltpu.VMEM((1, w), F32)],
        compiler_params=_params("arbitrary"),
        name="gate_cumsum",
    )(f, b)


def _online_softmax_step(u, off, vt, m_ref, l_ref, acc_ref, idx):
    m_prev = m_ref[idx]
    m_new = jnp.maximum(m_prev, jnp.max(u, axis=0, keepdims=True) + off)
    alpha = jnp.exp2(m_prev - m_new)
    p = jnp.exp2(u - (m_new - off))
    if l_ref is not None:
        l_ref[idx] = alpha * l_ref[idx] + jnp.sum(p, axis=0, keepdims=True)
    acc_ref[idx] = alpha * acc_ref[idx] + jnp.dot(vt, p.astype(BF16), preferred_element_type=F32)
    m_ref[idx] = m_new


def _with_ones_row(vt):
    ones = jnp.ones(vt.shape[:-2] + (1, vt.shape[-1]), vt.dtype)
    zeros = jnp.zeros(vt.shape[:-2] + (V7X_BF16_SUBLANES - 1, vt.shape[-1]), vt.dtype)
    return jnp.concatenate([vt, ones, zeros], axis=-2)


def _diag_chunks(tq, tk, cw):
    return [(b, cc) for b in range(tq // tk) for cc in range(tq // cw)
            if (cc + 1) * cw - 1 >= b * tk]


def _body_chunks(tq, tk, cw):
    return [(b, cc) for b in range(tq // tk) for cc in range(tq // cw)]


def _prime_scores(first_steps, qk, u_ref):
    for n in range(u_ref.shape[0] - 1):
        for mp, tile in enumerate(qk(first_steps[n])):
            u_ref[n, mp] = tile


def _pipelined_run(steps, next_steps, qk, fold, u_ref):
    n_steps = len(steps)
    ring = u_ref.shape[0]
    lookahead = ring - 1
    assert n_steps > lookahead and (next_steps is None or n_steps % ring == 0)
    for n, step in enumerate(steps):
        ahead = n + lookahead
        later = steps[ahead] if ahead < n_steps else (
            None if next_steps is None else next_steps[ahead - n_steps])
        if later is not None:
            for mp, tile in enumerate(qk(later)):
                u_ref[ahead % ring, mp] = tile
        fold(step, [u_ref[n % ring, mp] for mp in range(u_ref.shape[1])])


def _fox_kernel(base_ref, qt_ref, k_ref, vt_ref, o_ref, m_ref, acc_ref, u_ref, qa_ref,
                *, tq, tk, cw):
    hh = pl.program_id(0)
    i = pl.program_id(1)
    bpq = tq // tk
    dh = o_ref.shape[1]
    m_ref[...] = jnp.full_like(m_ref, NEG_INF)
    acc_ref[...] = jnp.zeros_like(acc_ref)
    qa_ref[:dh, :] = qt_ref[...]
    aug_row = lax.broadcasted_iota(jnp.int32, (qa_ref.shape[0] - dh, tq), 0)
    qa_ref[dh:, :] = (aug_row < FOX_AUG).astype(qa_ref.dtype)

    def qk(step):
        j, cc, _ = step
        return [jnp.dot(k_ref[j], qa_ref[:, cc * cw:(cc + 1) * cw], preferred_element_type=F32)]

    def fold(step, tiles):
        j, cc, key_rel0 = step
        u, = tiles
        if key_rel0 is not None and key_rel0 + tk - 1 > cc * cw:
            key = lax.broadcasted_iota(jnp.int32, (tk, cw), 0) + key_rel0
            qry = lax.broadcasted_iota(jnp.int32, (tk, cw), 1) + cc * cw
            u = jnp.where(key <= qry, u, NEG_INF)
        _online_softmax_step(u, base_ref[hh, j], vt_ref[j], m_ref, None, acc_ref,
                             (slice(None), slice(cc * cw, (cc + 1) * cw)))

    def full_steps(jj):
        return [(jj * bpq + b, cc, None) for b, cc in _body_chunks(tq, tk, cw)]

    diag_steps = [(i * bpq + b, cc, b * tk) for b, cc in _diag_chunks(tq, tk, cw)]
    lookahead = u_ref.shape[0] - 1
    assert _diag_chunks(tq, tk, cw)[:lookahead] == _body_chunks(tq, tk, cw)[:lookahead]
    _prime_scores(full_steps(0), qk, u_ref)

    def body(jj, carry):
        _pipelined_run(full_steps(jj), full_steps(jj + 1), qk, fold, u_ref)
        return carry

    lax.fori_loop(0, i, body, 0)
    _pipelined_run(diag_steps, None, qk, fold, u_ref)
    o_ref[...] = (acc_ref[:dh, :] / acc_ref[dh:dh + 1, :]).T.astype(o_ref.dtype)


def fox_attention(base, qt, k, vt, tq, tk):
    h, _, s = qt.shape
    kd = k.shape[-1]
    dvp = vt.shape[2]
    dh = dvp - V7X_BF16_SUBLANES
    nkb = s // tk
    cw = min(ATT_CW, tq)
    once = pl.Buffered(1)
    return pl.pallas_call(
        functools.partial(_fox_kernel, tq=tq, tk=tk, cw=cw),
        out_shape=jax.ShapeDtypeStruct((s, h * dh), BF16),
        grid=(h, s // tq),
        in_specs=[pl.BlockSpec(memory_space=pltpu.SMEM),
                  pl.BlockSpec((None, dh, tq), lambda hh, i: (hh, 0, i)),
                  pl.BlockSpec((None, nkb, tk, kd), lambda hh, i: (hh, 0, 0, 0),
                               pipeline_mode=once),
                  pl.BlockSpec((None, nkb, dvp, tk), lambda hh, i: (hh, 0, 0, 0),
                               pipeline_mode=once)],
        out_specs=pl.BlockSpec((tq, dh), lambda hh, i: (i, hh)),
        scratch_shapes=[pltpu.VMEM((1, tq), F32), pltpu.VMEM((dvp, tq), F32),
                        pltpu.VMEM((FOX_LOOKAHEAD + 1, 1, tk, cw), F32),
                        pltpu.VMEM((kd, tq), BF16)],
        compiler_params=_params("parallel", "arbitrary"),
        name="fox_attention",
    )(base, qt, k, vt)


def _diff_kernel(slope_ref, qt_ref, k_ref, vt_ref, lam_ref, g_ref, o_ref,
                 m_ref, l_ref, acc_ref, u_ref, bt_ref, bd_ref, *, tq, tk, cw, lam_init):
    hh = pl.program_id(0)
    i = pl.program_id(1)
    bpq = tq // tk
    beta = slope_ref[hh] * LOG2E
    m_ref[...] = jnp.full_like(m_ref, NEG_INF)
    l_ref[...] = jnp.zeros_like(l_ref)
    acc_ref[...] = jnp.zeros_like(acc_ref)
    dv = o_ref.shape[1]
    ones_row = acc_ref.shape[1] > dv
    diag_chunks = _diag_chunks(tq, tk, cw)

    @pl.when(i == 0)
    def _():
        key_i = lax.broadcasted_iota(jnp.int32, (tk, tq), 0)
        qry_i = lax.broadcasted_iota(jnp.int32, (tk, tq), 1)
        bt_ref[...] = (key_i - qry_i).astype(F32) * beta
        for n, (b, cc) in enumerate(diag_chunks):
            key = lax.broadcasted_iota(jnp.int32, (tk, cw), 0) + b * tk
            qry = lax.broadcasted_iota(jnp.int32, (tk, cw), 1) + cc * cw
            bias = jnp.abs(key - qry).astype(F32) * (-beta)
            bd_ref[n] = jnp.where(key // CHUNK <= qry // CHUNK, bias, NEG_INF)

    def qk(step):
        j, cc, _ = step
        lanes = slice(cc * cw, (cc + 1) * cw)
        return [jnp.dot(k_ref[mp, j], qt_ref[mp, :, lanes], preferred_element_type=F32)
                for mp in range(2)]

    def fold(step, u):
        j, cc, key_rel0 = step
        lanes = slice(cc * cw, (cc + 1) * cw)
        if key_rel0 is None:
            off = (j * tk - i * tq).astype(F32) * beta
            bias = bt_ref[:, lanes]
        else:
            off = 0.0
            bias = bd_ref[diag_chunks.index((key_rel0 // tk, cc))]
        vt = vt_ref[j]
        for mp in range(2):
            _online_softmax_step(u[mp] + bias, off, vt, m_ref, None if ones_row else l_ref,
                                 acc_ref, (mp, slice(None), lanes))

    def full_steps(jj):
        return [(jj * bpq + b, cc, None) for b, cc in _body_chunks(tq, tk, cw)]

    diag_steps = [(i * bpq + b, cc, b * tk) for b, cc in _diag_chunks(tq, tk, cw)]
    lookahead = u_ref.shape[0] - 1
    assert _diag_chunks(tq, tk, cw)[:lookahead] == _body_chunks(tq, tk, cw)[:lookahead]
    _prime_scores(full_steps(0), qk, u_ref)

    def body(jj, carry):
        _pipelined_run(full_steps(jj), full_steps(jj + 1), qk, fold, u_ref)
        return carry

    lax.fori_loop(0, i, body, 0)
    _pipelined_run(diag_steps, None, qk, fold, u_ref)

    lv = lam_ref[...]
    lam = (jnp.exp(jnp.sum(lv[0:1] * lv[1:2], axis=-1, keepdims=True))
           - jnp.exp(jnp.sum(lv[2:3] * lv[3:4], axis=-1, keepdims=True)) + lam_init)
    den = [acc_ref[mp, dv:dv + 1, :] if ones_row else l_ref[mp] for mp in range(2)]
    o = acc_ref[0, :dv, :] / den[0] - lam * (acc_ref[1, :dv, :] / den[1])
    ms = jnp.mean(o * o, axis=0, keepdims=True)
    on = (o * lax.rsqrt(ms + EPS)).T
    o_ref[...] = (on * g_ref[...] * (1.0 - lam_init)).astype(o_ref.dtype)


def diff_attention(qt, k, vt, lam_vecs, g_subln, tq, tk, lam_init):
    h, _, dh, s = qt.shape
    kd = k.shape[-1]
    dvp = vt.shape[2]
    dv = g_subln.shape[-1]
    nkb = s // tk
    slopes = jnp.exp2(-8.0 * jnp.arange(1, h + 1, dtype=F32) / h)
    cw = min(ATT_CW, tq)
    once = pl.Buffered(1)
    return pl.pallas_call(
        functools.partial(_diff_kernel, tq=tq, tk=tk, cw=cw, lam_init=lam_init),
        out_shape=jax.ShapeDtypeStruct((s, h * dv), BF16),
        grid=(h, s // tq),
        in_specs=[pl.BlockSpec(memory_space=pltpu.SMEM),
                  pl.BlockSpec((None, 2, dh, tq), lambda hh, i: (hh, 0, 0, i)),
                  pl.BlockSpec((None, 2, nkb, tk, kd), lambda hh, i: (hh, 0, 0, 0, 0),
                               pipeline_mode=once),
                  pl.BlockSpec((None, nkb, dvp, tk), lambda hh, i: (hh, 0, 0, 0),
                               pipeline_mode=once),
                  pl.BlockSpec((4, dh), lambda hh, i: (0, 0)),
                  pl.BlockSpec((1, dv), lambda hh, i: (0, 0))],
        out_specs=pl.BlockSpec((tq, dv), lambda hh, i: (i, hh)),
        scratch_shapes=[pltpu.VMEM((2, 1, tq), F32), pltpu.VMEM((2, 1, tq), F32),
                        pltpu.VMEM((2, dvp, tq), F32),
                        pltpu.VMEM((DIFF_LOOKAHEAD + 1, 2, tk, cw), F32),
                        pltpu.VMEM((tk, tq), F32),
                        pltpu.VMEM((len(_diag_chunks(tq, tk, cw)), tk, cw), F32)],
        compiler_params=_params("parallel", "arbitrary"),
        name="diff_attention",
    )(slopes, qt, k, vt, lam_vecs.astype(F32), g_subln.reshape(1, dv).astype(F32))


def _ffn(h, g, w_in, w_out, lead):
    a = rmsnorm(h, g, BF16)
    hid = swiglu_in(a, w_in, lead)
    return matmul_residual(hid, w_out, h, 0.5, lead)


def _fox_mixer(h, g, w, b_f, w_out, l, tq, tk):
    s, d = h.shape
    n_heads = d // HEAD_DIM
    nkb = s // tk
    hn = rmsnorm(h, g, BF16)
    lead = (l,)
    w_f = jnp.zeros((d, V7X_LANES), BF16).at[:, :n_heads].set(w[l, :, 3 * d:])
    b_pad = jnp.zeros((1, V7X_LANES), F32).at[0, :n_heads].set(b_f.astype(F32))
    nd3, nd_base = gate_cumsum(matmul(hn, w_f, F32), b_pad, tk)
    nd3 = nd3[:, :, :n_heads]
    nd_base = nd_base[:, 0, :n_heads].T
    aug = jnp.pad(nd3.transpose(2, 1, 0), ((0, 0), (0, 0), (0, V7X_MXU_DIM - HEAD_DIM - FOX_AUG)))
    aug = aug.reshape(n_heads, nkb, tk, V7X_MXU_DIM - HEAD_DIM)
    qt = project(hn, w, "qt", lead=lead, col0=0, ncols=d, scale=QK_SCALE2)
    qt = qt.reshape(n_heads, HEAD_DIM, s)
    k = project(hn, w, "k", lead=lead, col0=d, ncols=d, tk=tk, aug=aug)
    vt = project(hn, w, "vt", lead=lead, col0=2 * d, ncols=d, tk=tk,
                 pad_rows=V7X_BF16_SUBLANES)
    o = fox_attention(nd_base, qt, k, vt, tq, tk)
    return matmul_residual(o, w_out, h, 1.0, lead)


def _diff_mixer(h, g, k, vt, w_q, lam_vecs, g_subln, w_out, j, layer_idx, tq, tk):
    s, d = h.shape
    n_heads = d // (2 * HEAD_DIM)
    lam_init = 0.8 - 0.6 * math.exp(-0.3 * (layer_idx - 1))
    hn = rmsnorm(h, g, BF16)
    qt = project(hn, w_q, "qt", lead=(j,), scale=QK_SCALE2).reshape(n_heads, 2, HEAD_DIM, s)
    o = diff_attention(qt, k, vt, lam_vecs, g_subln, tq, tk, lam_init)
    return matmul_residual(o, w_out, h, 1.0, (j,))


def kernel(x, g_ffn, w_ffn_in, w_ffn_out, g_mix, w_in_a, b_f, w_out_a, g_kv, w_kv, w_q_b, lam_b,
           g_subln, w_out_b, g_final):
    b, s, d = x.shape
    depth = g_ffn.shape[0]
    n_a = w_in_a.shape[0]
    tq_fox = _tile(s, FOX_TQ)
    tq_diff = _tile(s, DIFF_TQ)
    tk = _tile(min(tq_fox, tq_diff), ATT_TK)
    w_ffn_in, w_ffn_out, w_in_a, w_out_a, w_q_b, w_out_b = (
        w.astype(BF16) for w in (w_ffn_in, w_ffn_out, w_in_a, w_out_a, w_q_b, w_out_b))
    outs = []
    for bi in range(b):
        h = x.reshape(s, d) if b == 1 else x[bi]
        k_sh = vt_sh = None
        for l in range(depth):
            h = _ffn(h, g_ffn[l, 0], w_ffn_in, w_ffn_out, (l, 0))
            if l < n_a:
                h = _fox_mixer(h, g_mix[l], w_in_a, b_f[l], w_out_a, l, tq_fox, tk)
            else:
                j = l - n_a
                h = _diff_mixer(h, g_mix[l], k_sh, vt_sh, w_q_b, lam_b[j], g_subln[j],
                                w_out_b, j, l + 1, tq_diff, tk)
            h = _ffn(h, g_ffn[l, 1], w_ffn_in, w_ffn_out, (l, 1))
            if l == n_a - 1:
                n_b = d // (2 * HEAD_DIM)
                nkb = s // tk
                kvn = rmsnorm(h, g_kv, BF16)
                wkv = w_kv.astype(BF16)
                k_sh = project(kvn, wkv, "k", col0=0, ncols=d, tk=tk)
                k_sh = k_sh.reshape(n_b, 2, nkb, tk, HEAD_DIM)
                vt_sh = project(kvn, wkv, "vt", col0=d, ncols=d, width=2 * HEAD_DIM, tk=tk,
                                pad_rows=DIFF_ONES_PAD)
        outs.append(rmsnorm(h, g_final, x.dtype))
    return outs[0].reshape(b, s, d) if b == 1 else jnp.stack(outs, axis=0)
```
